```python
import jax
import jax.numpy as jnp
from jax import lax
import numpy as np

D_MODEL = 1024
BATCH = 32
SEQ = 2048
DEPTH = 2
DEC_BATCH = 128
DEC_SEQ = 4
PAST_LEN = 16384
PAGE_SIZE = 128

N_EVEN = (DEPTH + 1) // 2
N_ODD = DEPTH // 2
ROPE_THETA = 500000.0
EPS = 1e-6

A_CH = D_MODEL // 2
CONV_A_WIDTH = 31

DH_B = 64
HB = (D_MODEL // 2) // DH_B
HKV_B = 2
ROT_B = DH_B // 4
H_IDX = 4
D_IDX = 64
ROT_IDX = D_IDX // 4
TOPK_MAX = 256
IDX_W_SCALE = (H_IDX * D_IDX) ** -0.5
SPARSE_QBLOCK = 64

VD_C = 64
HC = (D_MODEL // 2) // VD_C
NOPE_C = 64
ROPE_C = 32
DQK_C = NOPE_C + ROPE_C
Q_LORA = 3 * D_MODEL // 8
KV_LORA = D_MODEL // 4
MLA_SCALE = DQK_C ** -0.5
ATTN_QBLOCK = 128

D_CH = D_MODEL // 2
D_GROUPS = 8
CHUNK = 128

D_FF = 11 * D_MODEL // 4
FFN_CONV_WIDTH = 3

E_COLS = (2 * A_CH, HB * DH_B, HKV_B * DH_B, HKV_B * DH_B, H_IDX * D_IDX, D_IDX, H_IDX)
IN_E = sum(E_COLS)
MIX_E = A_CH + HB * DH_B
IN_O = Q_LORA + KV_LORA + ROPE_C + 2 * D_CH
MIX_O = HC * VD_C + D_CH

kernel_name = 'hybrid_convdsa_mla_gmlp_decoder_step'


def rms_norm(x, g):
    xf = x.astype(jnp.float32)
    y = xf * lax.rsqrt(jnp.mean(xf * xf, axis=-1, keepdims=True) + EPS)
    return (y * g.astype(jnp.float32)).astype(x.dtype)


def layer_norm(x, g, b):
    xf = x.astype(jnp.float32)
    mu = jnp.mean(xf, axis=-1, keepdims=True)
    var = jnp.mean(jnp.square(xf - mu), axis=-1, keepdims=True)
    y = (xf - mu) * lax.rsqrt(var + EPS)
    return (y * g.astype(jnp.float32) + b.astype(jnp.float32)).astype(x.dtype)


def apply_rope(x, pos, n_rot):
    half = n_rot // 2
    inv_freq = jnp.power(jnp.float32(ROPE_THETA), -jnp.arange(half, dtype=jnp.float32) * (2.0 / n_rot))
    ang = pos.astype(jnp.float32)[:, None] * inv_freq[None, :]
    cos = jnp.cos(ang)[:, None, :]
    sin = jnp.sin(ang)[:, None, :]
    x1 = x[..., :half].astype(jnp.float32)
    x2 = x[..., half:n_rot].astype(jnp.float32)
    rot = jnp.concatenate([x1 * cos - x2 * sin, x2 * cos + x1 * sin], axis=-1).astype(x.dtype)
    return jnp.concatenate([rot, x[..., n_rot:]], axis=-1)


def causal_dwconv(x, buf, w, b):
    xp = jnp.concatenate([buf, x], axis=1)
    y = lax.conv_general_dilated(xp, w[:, None, :], (1,), 'VALID',
                                 dimension_numbers=('NWC', 'WIO', 'NWC'),
                                 feature_group_count=x.shape[-1])
    return y + b, xp[:, xp.shape[1] - (w.shape[0] - 1):]


def conv_ffn(h, buf, w_up, cw, cb, w_down):
    u, new_buf = causal_dwconv(h @ w_up, buf, cw, cb)
    a, g = u[..., :D_FF], u[..., D_FF:]
    return (jax.nn.silu(g) * a) @ w_down, new_buf


def even_mixer_proj(h, pos, w_in, q_norm, k_norm):
    n, t, _ = h.shape
    cuts = np.cumsum(E_COLS[:-1]).tolist()
    glu, q, k, v, qi, ki, wi = jnp.split(h @ w_in, cuts, axis=-1)
    a = glu[..., :A_CH] * jax.nn.sigmoid(glu[..., A_CH:])
    q = apply_rope(rms_norm(q.reshape(n, t, HB, DH_B), q_norm), pos, ROT_B)
    k = apply_rope(rms_norm(k.reshape(n, t, HKV_B, DH_B), k_norm), pos, ROT_B)
    v = v.reshape(n, t, HKV_B, DH_B)
    qi = apply_rope(qi.reshape(n, t, H_IDX, D_IDX), pos, ROT_IDX)
    ki = apply_rope(ki[:, :, None, :], pos, ROT_IDX)[:, :, 0]
    wi = wi.astype(jnp.float32) * IDX_W_SCALE
    return a, q, k, v, qi, ki, wi


def conformer_tail(a, buf, conv_w, conv_b, ln_g, ln_b):
    c, new_buf = causal_dwconv(a, buf, conv_w, conv_b)
    return jax.nn.silu(layer_norm(c, ln_g, ln_b)), new_buf


def dsa_select(qi, wi, ki, q_pos, k_pos, n_sel):
    logits = jnp.einsum('thd,sd->ths', qi, ki, preferred_element_type=jnp.float32)
    score = jnp.einsum('th,ths->ts', wi, jax.nn.relu(logits))
    score = jnp.where(k_pos[None, :] <= q_pos[:, None], score, -jnp.inf)
    _, sel = lax.top_k(score, n_sel)
    return sel, sel <= q_pos[:, None]


def dsa_attend(q, k_sel, v_sel, valid):
    tq = q.shape[0]
    qg = q.reshape(tq, HKV_B, HB // HKV_B, DH_B)
    s = jnp.einsum('tgrd,tkgd->tgrk', qg, k_sel, preferred_element_type=jnp.float32) * (DH_B ** -0.5)
    s = jnp.where(valid[:, None, None, :], s, -jnp.inf)
    p = jax.nn.softmax(s, axis=-1).astype(v_sel.dtype)
    return jnp.einsum('tgrk,tkgd->tgrd', p, v_sel).reshape(tq, HB * DH_B)


def dsa_prompt(q, k, v, qi, ki, wi):
    n, s = q.shape[:2]
    n_sel = min(TOPK_MAX, s // 4)
    k_pos = jnp.arange(s)

    def block(i):
        t0 = i * SPARSE_QBLOCK
        q_pos = t0 + jnp.arange(SPARSE_QBLOCK)
        sl = lambda arr: lax.dynamic_slice_in_dim(arr, t0, SPARSE_QBLOCK, axis=1)

        def per_seq(qb, qib, wib, kk, vv, kik):
            sel, valid = dsa_select(qib, wib, kik, q_pos, k_pos, n_sel)
            return dsa_attend(qb, kk[sel], vv[sel], valid)
        return jax.vmap(per_seq)(sl(q), sl(qi), sl(wi), k, v, ki)

    out = lax.map(block, jnp.arange(s // SPARSE_QBLOCK))
    return jnp.moveaxis(out, 0, 1).reshape(n, s, HB * DH_B)


def dsa_sample(q, k, v, qi, ki, wi, cache_k, cache_v, cache_ki, page_table, li):
    n, t = q.shape[:2]
    past = page_table.shape[1] * PAGE_SIZE
    total = past + t
    n_sel = min(TOPK_MAX, total // 4)
    q_pos = past + jnp.arange(t)
    k_pos = jnp.arange(total)

    def per_seq(args):
        qs, kn, vn, qis, kin, wis, pt = args
        ki_all = jnp.concatenate([cache_ki[li, pt].reshape(past, D_IDX), kin], axis=0)
        sel, valid = dsa_select(qis, wis, ki_all, q_pos, k_pos, n_sel)
        in_past = (sel < past)[..., None, None]
        sp = jnp.minimum(sel, past - 1)
        phys = pt[sp // PAGE_SIZE]
        off = sp % PAGE_SIZE
        sn = jnp.clip(sel - past, 0, t - 1)
        k_sel = jnp.where(in_past, cache_k[li, phys, off], kn[sn])
        v_sel = jnp.where(in_past, cache_v[li, phys, off], vn[sn])
        return dsa_attend(qs, k_sel, v_sel, valid)

    return lax.map(per_seq, (q, k, v, qi, ki, wi, page_table))


def odd_mixer_proj(h, pos, w_in, q_a_norm, w_qb, kv_a_norm, q_norm, gl_g, gl_b):
    n, t, _ = h.shape
    qa, ckv, kpe, z = jnp.split(h @ w_in, [Q_LORA, Q_LORA + KV_LORA, Q_LORA + KV_LORA + ROPE_C], axis=-1)
    q = (rms_norm(qa, q_a_norm) @ w_qb).reshape(n, t, HC, DQK_C)
    q = apply_rope(rms_norm(q, q_norm), pos, ROPE_C)
    ckv = rms_norm(ckv, kv_a_norm)
    z = jax.nn.gelu(z)
    u = z[..., :D_CH]
    v = layer_norm(z[..., D_CH:], gl_g, gl_b)
    return q, ckv, kpe, u, v


def mla_keys(ckv, kpe, pos, w_uk, k_norm):
    k_nope = jnp.einsum('...sc,chd->...shd', ckv, w_uk)
    k_pe = jnp.broadcast_to(kpe[..., None, :], k_nope.shape[:-1] + (ROPE_C,))
    k = rms_norm(jnp.concatenate([k_pe, k_nope], axis=-1), k_norm)
    return apply_rope(k, pos, ROPE_C)


def mla_attend(q, k, ckv, w_uv, mask):
    s = jnp.einsum('thd,shd->hts', q, k, preferred_element_type=jnp.float32) * MLA_SCALE
    p = jax.nn.softmax(jnp.where(mask[None], s, -jnp.inf), axis=-1).astype(ckv.dtype)
    o_lat = jnp.einsum('hts,sc->thc', p, ckv)
    return jnp.einsum('thc,chv->thv', o_lat, w_uv).reshape(q.shape[0], HC * VD_C)


def mla_prompt(q, k, ckv, w_uv):
    n, s = q.shape[:2]
    k_pos = jnp.arange(s)

    def block(i):
        t0 = i * ATTN_QBLOCK
        qb = lax.dynamic_slice_in_dim(q, t0, ATTN_QBLOCK, axis=1)
        mask = k_pos[None, :] <= (t0 + jnp.arange(ATTN_QBLOCK))[:, None]
        return jax.vmap(lambda qq, kk, cc: mla_attend(qq, kk, cc, w_uv, mask))(qb, k, ckv)

    out = lax.map(block, jnp.arange(s // ATTN_QBLOCK))
    return jnp.moveaxis(out, 0, 1).reshape(n, s, HC * VD_C)


def mla_sample(q, ckv, kpe, cache_ckv, cache_kpe, page_table, li, w_uk, k_norm, w_uv):
    t = q.shape[1]
    past = page_table.shape[1] * PAGE_SIZE
    k_pos = jnp.arange(past + t)
    mask = k_pos[None, :] <= (past + jnp.arange(t))[:, None]

    def per_seq(args):
        qs, cn, pn, pt = args
        c_all = jnp.concatenate([cache_ckv[li, pt].reshape(past, KV_LORA), cn], axis=0)
        p_all = jnp.concatenate([cache_kpe[li, pt].reshape(past, ROPE_C), pn], axis=0)
        k = mla_keys(c_all, p_all, k_pos, w_uk, k_norm)
        return mla_attend(qs, k, c_all, w_uv, mask)

    return lax.map(per_seq, (q, ckv, kpe, page_table))


def gmlp_spatial(u, v, w_s, b_s):
    n, t, _ = u.shape
    nc = -(-t // CHUNK)
    vp = jnp.pad(v, ((0, 0), (0, nc * CHUNK - t), (0, 0))).reshape(n, nc, CHUNK, D_GROUPS, D_CH // D_GROUPS)
    w = w_s * jnp.tril(jnp.ones((CHUNK, CHUNK), w_s.dtype))[None]
    mix = jnp.einsum('gts,ncsgd->nctgd', w, vp) + b_s.T[None, None, :, :, None]
    return u * mix.reshape(n, nc * CHUNK, D_CH)[:, :t]


def setup_inputs(seed: int = 0) -> dict:
    key = jax.random.key(seed)
    keys = iter(jax.random.split(key, 48))

    def nrm(shape, scale=1.0):
        return jax.random.normal(next(keys), shape, jnp.float32) * scale

    def gain(shape):
        return 1.0 + nrm(shape, 0.02)

    n_pages = PAST_LEN // PAGE_SIZE
    n_used = DEC_BATCH * n_pages
    n_pool = n_used + n_used // 4
    page_table = jax.random.permutation(next(keys), n_pool)[:n_used].reshape(DEC_BATCH, n_pages).astype(jnp.int32)
    return {
        'x_prompt': nrm((BATCH, SEQ, D_MODEL)),
        'x_sample': nrm((DEC_BATCH, DEC_SEQ, D_MODEL)),
        'cache_dsa_k': nrm((N_EVEN, n_pool, PAGE_SIZE, HKV_B, DH_B)),
        'cache_dsa_v': nrm((N_EVEN, n_pool, PAGE_SIZE, HKV_B, DH_B)),
        'cache_dsa_kidx': nrm((N_EVEN, n_pool, PAGE_SIZE, D_IDX)),
        'state_conv_a': nrm((N_EVEN, DEC_BATCH, CONV_A_WIDTH - 1, A_CH), 0.5),
        'cache_mla_ckv': nrm((N_ODD, n_pool, PAGE_SIZE, KV_LORA)),
        'cache_mla_kpe': nrm((N_ODD, n_pool, PAGE_SIZE, ROPE_C)),
        'state_ffn_conv': nrm((DEPTH, DEC_BATCH, FFN_CONV_WIDTH - 1, 2 * D_FF), 0.5),
        'page_table': page_table,
        'norm_mix': gain((DEPTH, D_MODEL)),
        'norm_ffn': gain((DEPTH, D_MODEL)),
        'w_in_e': nrm((N_EVEN, D_MODEL, IN_E), D_MODEL ** -0.5),
        'conv_a_w': nrm((N_EVEN, CONV_A_WIDTH, A_CH), CONV_A_WIDTH ** -0.5),
        'conv_a_b': nrm((N_EVEN, A_CH), 0.01),
        'conv_a_ln_g': gain((N_EVEN, A_CH)),
        'conv_a_ln_b': nrm((N_EVEN, A_CH), 0.01),
        'q_norm_b': gain((N_EVEN, DH_B)),
        'k_norm_b': gain((N_EVEN, DH_B)),
        'w_out_e': nrm((N_EVEN, MIX_E, D_MODEL), MIX_E ** -0.5),
        'w_in_o': nrm((N_ODD, D_MODEL, IN_O), D_MODEL ** -0.5),
        'q_a_norm': gain((N_ODD, Q_LORA)),
        'w_qb': nrm((N_ODD, Q_LORA, HC * DQK_C), Q_LORA ** -0.5),
        'kv_a_norm': gain((N_ODD, KV_LORA)),
        'w_uk': nrm((N_ODD, KV_LORA, HC, NOPE_C), KV_LORA ** -0.5),
        'w_uv': nrm((N_ODD, KV_LORA, HC, VD_C), KV_LORA ** -0.5),
        'q_norm_c': gain((N_ODD, DQK_C)),
        'k_norm_c': gain((N_ODD, DQK_C)),
        'gmlp_ln_g': gain((N_ODD, D_CH)),
        'gmlp_ln_b': nrm((N_ODD, D_CH), 0.01),
        'w_spatial': nrm((N_ODD, D_GROUPS, CHUNK, CHUNK), CHUNK ** -0.5),
        'b_spatial': gain((N_ODD, D_GROUPS, CHUNK)),
        'w_out_o': nrm((N_ODD, MIX_O, D_MODEL), MIX_O ** -0.5),
        'w_up': nrm((DEPTH, D_MODEL, 2 * D_FF), D_MODEL ** -0.5),
        'ffn_conv_w': nrm((DEPTH, FFN_CONV_WIDTH, 2 * D_FF), FFN_CONV_WIDTH ** -0.5),
        'ffn_conv_b': nrm((DEPTH, 2 * D_FF), 0.01),
        'w_down': nrm((DEPTH, D_FF, D_MODEL), D_FF ** -0.5),
    }


def reference(x_prompt, x_sample, cache_dsa_k, cache_dsa_v, cache_dsa_kidx, state_conv_a,
              cache_mla_ckv, cache_mla_kpe, state_ffn_conv, page_table,
              norm_mix, norm_ffn,
              w_in_e, conv_a_w, conv_a_b, conv_a_ln_g, conv_a_ln_b, q_norm_b, k_norm_b, w_out_e,
              w_in_o, q_a_norm, w_qb, kv_a_norm, w_uk, w_uv, q_norm_c, k_norm_c,
              gmlp_ln_g, gmlp_ln_b, w_spatial, b_spatial, w_out_o,
              w_up, ffn_conv_w, ffn_conv_b, w_down):
    bsz, seq = x_prompt.shape[:2]
    dec_seq = x_sample.shape[1]
    past = page_table.shape[1] * PAGE_SIZE
    pos_p = jnp.arange(seq)
    pos_s = past + jnp.arange(dec_seq)
    yp, ys = x_prompt, x_sample
    pk, pv, pki, pca, pckv, pkpe, pff = [], [], [], [], [], [], []
    sk, sv, ski, sca, sckv, skpe, sgv, sff = [], [], [], [], [], [], [], []

    for layer in range(DEPTH):
        hp = rms_norm(yp, norm_mix[layer])
        hs = rms_norm(ys, norm_mix[layer])
        if layer % 2 == 0:
            e = layer // 2
            ap, qp, kp, vp, qip, kip, wip = even_mixer_proj(hp, pos_p, w_in_e[e], q_norm_b[e], k_norm_b[e])
            a_s, qs, ks, vs, qis, kis, wis = even_mixer_proj(hs, pos_s, w_in_e[e], q_norm_b[e], k_norm_b[e])
            zero_buf = jnp.zeros((bsz, CONV_A_WIDTH - 1, A_CH), ap.dtype)
            cp, buf_p = conformer_tail(ap, zero_buf, conv_a_w[e], conv_a_b[e], conv_a_ln_g[e], conv_a_ln_b[e])
            cs, buf_s = conformer_tail(a_s, state_conv_a[e], conv_a_w[e], conv_a_b[e], conv_a_ln_g[e], conv_a_ln_b[e])
            bp = dsa_prompt(qp, kp, vp, qip, kip, wip)
            bs = dsa_sample(qs, ks, vs, qis, kis, wis, cache_dsa_k, cache_dsa_v, cache_dsa_kidx, page_table, e)
            yp = yp + jnp.concatenate([cp, bp], axis=-1) @ w_out_e[e]
            ys = ys + jnp.concatenate([cs, bs], axis=-1) @ w_out_e[e]
            pk.append(kp); pv.append(vp); pki.append(kip); pca.append(buf_p)
            sk.append(ks); sv.append(vs); ski.append(kis); sca.append(buf_s)
        else:
            o = layer // 2
            qp, cp_, pp, up, vp = odd_mixer_proj(hp, pos_p, w_in_o[o], q_a_norm[o], w_qb[o], kv_a_norm[o],
                                                 q_norm_c[o], gmlp_ln_g[o], gmlp_ln_b[o])
            qs, cs_, ps, us, vs = odd_mixer_proj(hs, pos_s, w_in_o[o], q_a_norm[o], w_qb[o], kv_a_norm[o],
                                                 q_norm_c[o], gmlp_ln_g[o], gmlp_ln_b[o])
            kp = mla_keys(cp_, pp, pos_p, w_uk[o], k_norm_c[o])
            mp = mla_prompt(qp, kp, cp_, w_uv[o])
            ms = mla_sample(qs, cs_, ps, cache_mla_ckv, cache_mla_kpe, page_table, o, w_uk[o], k_norm_c[o], w_uv[o])
            gp = gmlp_spatial(up, vp, w_spatial[o], b_spatial[o])
            gs = gmlp_spatial(us, vs, w_spatial[o], b_spatial[o])
            yp = yp + jnp.concatenate([mp, gp], axis=-1) @ w_out_o[o]
            ys = ys + jnp.concatenate([ms, gs], axis=-1) @ w_out_o[o]
            pckv.append(cp_); pkpe.append(pp)
            sckv.append(cs_); skpe.append(ps); sgv.append(vs)

        hp = rms_norm(yp, norm_ffn[layer])
        hs = rms_norm(ys, norm_ffn[layer])
        zero_ffn = jnp.zeros((bsz, FFN_CONV_WIDTH - 1, 2 * D_FF), hp.dtype)
        fp, fbuf_p = conv_ffn(hp, zero_ffn, w_up[layer], ffn_conv_w[layer], ffn_conv_b[layer], w_down[layer])
        fs, fbuf_s = conv_ffn(hs, state_ffn_conv[layer], w_up[layer], ffn_conv_w[layer], ffn_conv_b[layer], w_down[layer])
        yp = yp + fp
        ys = ys + fs
        pff.append(fbuf_p)
        sff.append(fbuf_s)

    return (yp, ys,
            jnp.stack(pk), jnp.stack(pv), jnp.stack(pki), jnp.stack(pca),
            jnp.stack(pckv), jnp.stack(pkpe), jnp.stack(pff),
            jnp.stack(sk), jnp.stack(sv), jnp.stack(ski), jnp.stack(sca),
            jnp.stack(sckv), jnp.stack(skpe), jnp.stack(sgv), jnp.stack(sff))
```

```python
import functools

import numpy as np
import jax
import jax.numpy as jnp
from jax import lax
from jax.experimental import pallas as pl
from jax.experimental.pallas import tpu as pltpu

F32 = jnp.float32
BF = jnp.bfloat16

D_MODEL = 1024
PAGE_SIZE = 128
ROPE_THETA = 500000.0
EPS = 1e-6

A_CH = D_MODEL // 2
CONV_A_WIDTH = 31

DH_B = 64
HB = (D_MODEL // 2) // DH_B
HKV_B = 2
ROT_B = DH_B // 4
H_IDX = 4
D_IDX = 64
ROT_IDX = D_IDX // 4
TOPK_MAX = 256
IDX_W_SCALE = (H_IDX * D_IDX) ** -0.5

VD_C = 64
HC = (D_MODEL // 2) // VD_C
NOPE_C = 64
ROPE_C = 32
DQK_C = NOPE_C + ROPE_C
Q_LORA = 3 * D_MODEL // 8
KV_LORA = D_MODEL // 4
MLA_SCALE = DQK_C ** -0.5

D_CH = D_MODEL // 2
D_GROUPS = 8
CHUNK = 128

D_FF = 11 * D_MODEL // 4
FFN_CONV_WIDTH = 3

LANES = 128
SUBLANES = 8
NEG = -1e30
VMEM_LIMIT = 56 * 1024 * 1024

E_A, E_G, E_Q, E_K, E_V, E_QI, E_KI, E_WI, E_END = 0, 512, 1024, 1536, 1664, 1792, 2048, 2176, 2304
O_QA, O_CKV, O_KPE, O_U, O_V, O_END = 0, 384, 640, 768, 1280, 1792


def _dot(a, b):
    return jnp.dot(a, b, preferred_element_type=F32)


def _dot_nt(a, b):
    return lax.dot_general(a, b, (((1,), (1,)), ((), ())), preferred_element_type=F32)


def _split2(x):
    hi = x.astype(BF)
    lo = (x - hi.astype(F32)).astype(BF)
    return hi, lo


def _dot2(x, m):
    hi, lo = _split2(x)
    return _dot(hi, m) + _dot(lo, m)


def _dot2_nt(m, x):
    hi, lo = _split2(x)
    return _dot_nt(m, hi) + _dot_nt(m, lo)


def _rms(x, g):
    return x * lax.rsqrt(jnp.mean(x * x, axis=-1, keepdims=True) + EPS) * g


def _rope128(x, c, sa, sb, half):
    return x * c + pltpu.roll(x, LANES - half, 1) * sa + pltpu.roll(x, half, 1) * sb


def _params(sem, vmem=VMEM_LIMIT):
    return pltpu.CompilerParams(dimension_semantics=sem, vmem_limit_bytes=vmem)


def _full(shape):
    n = len(shape)
    return pl.BlockSpec(shape, lambda *_: (0,) * n)


def _low_half(rows):
    return lax.broadcasted_iota(jnp.int32, (rows, LANES), 1) < (LANES // 2)


def _rope_tables(pos, n_rot, head_w):
    half = n_rot // 2
    inv_freq = jnp.power(jnp.float32(ROPE_THETA), -jnp.arange(half, dtype=F32) * (2.0 / n_rot))
    ang = pos.astype(F32)[:, None] * inv_freq[None, :]
    cos, sin = jnp.cos(ang), jnp.sin(ang)
    m = pos.shape[0]
    one = jnp.ones((m, head_w - n_rot), F32)
    zero = jnp.zeros((m, head_w - n_rot), F32)
    zh = jnp.zeros((m, half), F32)
    c = jnp.concatenate([cos, cos, one], axis=1)
    sa = jnp.concatenate([-sin, zh, zero], axis=1)
    sb = jnp.concatenate([zh, sin, zero], axis=1)
    rep = LANES // head_w
    return tuple(jnp.tile(t, (1, rep)) for t in (c, sa, sb))


def _indicator(width, group):
    lane = jnp.arange(width)[:, None] // group
    col = jnp.arange(LANES)[None, :]
    ind = (lane == col).astype(BF)
    return ind, ind.T


def _even_proj_body(x_ref, g_ref, w_ref, nrm_ref, ind_ref, indt_ref, c_ref, sa_ref, sb_ref,
                    a_ref, q_ref, k_ref, v_ref, ki_ref, wi_ref, qih_ref, qil_ref, kb_ref, vb_ref, kih_ref, kil_ref):
    h = _rms(x_ref[0], g_ref[...]).astype(BF)
    z = _dot(h, w_ref[...])
    a_ref[0] = z[:, E_A:E_G] * jax.nn.sigmoid(z[:, E_G:E_Q])
    c, sa, sb = c_ref[...], sa_ref[...], sb_ref[...]
    half = ROT_B // 2
    qk = z[:, E_Q:E_V]
    ssq = _dot2(qk * qk, ind_ref[...])
    r = lax.rsqrt(ssq / DH_B + EPS)
    qk = qk * _dot2(r, indt_ref[...]) * nrm_ref[...]
    for s in range(4):
        slab = _rope128(qk[:, s * LANES:(s + 1) * LANES], c, sa, sb, half)
        q_ref[0, :, s * LANES:(s + 1) * LANES] = (slab * (DH_B ** -0.5)).astype(BF)
    k = _rope128(qk[:, 4 * LANES:5 * LANES], c, sa, sb, half)
    k_ref[0] = k
    kb_ref[0] = k.astype(BF)
    v = z[:, E_V:E_QI]
    v_ref[0] = v
    vb_ref[0] = v.astype(BF)
    for s in range(2):
        lo = E_QI + s * LANES
        hi, lw = _split2(_rope128(z[:, lo:lo + LANES], c, sa, sb, half))
        qih_ref[0, :, s * LANES:(s + 1) * LANES] = hi
        qil_ref[0, :, s * LANES:(s + 1) * LANES] = lw
    ki = _rope128(z[:, E_KI:E_WI], c, sa, sb, half)
    ki_ref[0] = ki[:, :D_IDX]
    kih_ref[0], kil_ref[0] = _split2(ki)
    wi_ref[0] = z[:, E_WI:E_END] * IDX_W_SCALE


def _even_weights(w_in, q_norm, k_norm):
    cuts = [0, 2 * A_CH, 2 * A_CH + 512, 2 * A_CH + 640, 2 * A_CH + 768, 2 * A_CH + 1024, 2 * A_CH + 1088]
    glu = w_in[:, cuts[0]:cuts[1]]
    q = w_in[:, cuts[1]:cuts[2]]
    k = w_in[:, cuts[2]:cuts[3]]
    v = w_in[:, cuts[3]:cuts[4]]
    qi = w_in[:, cuts[4]:cuts[5]]
    ki = w_in[:, cuts[5]:cuts[6]]
    wi = w_in[:, cuts[6]:]
    q = q[:, _DSA_HEAD_PERM]
    pad = jnp.zeros((w_in.shape[0], LANES - H_IDX), w_in.dtype)
    w = jnp.concatenate([glu, q, k, v, qi, ki, ki, wi, pad], axis=1).astype(BF)
    nrm = jnp.concatenate([jnp.tile(q_norm, HB), jnp.tile(k_norm, HKV_B)])[None, :]
    return w, nrm


def _dsa_head_perm():
    j = np.arange(HB * DH_B)
    head = (j // LANES) + (HB // HKV_B) * ((j % LANES) // DH_B)
    return head * DH_B + (j % DH_B)


_DSA_HEAD_PERM = _dsa_head_perm()


def _even_proj(x, g, w, nrm, tables, tb):
    grp, t, _ = x.shape
    ind, indt = _indicator(5 * LANES, DH_B)
    blk = lambda width: pl.BlockSpec((1, tb, width), lambda i, j: (i, j, 0))
    tab = pl.BlockSpec((tb, LANES), lambda i, j: (j, 0))
    sds = lambda width, dt: jax.ShapeDtypeStruct((grp, t, width), dt)
    return pl.pallas_call(
        _even_proj_body,
        grid=(grp, t // tb),
        in_specs=[blk(D_MODEL), _full((1, D_MODEL)), _full(w.shape), _full(nrm.shape),
                  _full(ind.shape), _full(indt.shape), tab, tab, tab],
        out_specs=[blk(A_CH), blk(512), blk(LANES), blk(LANES), blk(D_IDX), blk(LANES),
                   blk(256), blk(256), blk(LANES), blk(LANES), blk(LANES), blk(LANES)],
        out_shape=[sds(A_CH, F32), sds(512, BF), sds(LANES, F32), sds(LANES, F32), sds(D_IDX, F32),
                   sds(LANES, F32), sds(256, BF), sds(256, BF), sds(LANES, BF), sds(LANES, BF),
                   sds(LANES, BF), sds(LANES, BF)],
        compiler_params=_params(("parallel", "parallel")),
        name="even_proj",
    )(x, g[None, :], w, nrm, ind, indt, *tables)


def _conv_tail_body(a_ref, prev_ref, w_ref, b_ref, g_ref, beta_ref, o_ref, buf, *, tb, pad, stride, rows, carry):
    j = pl.program_id(1)

    @pl.when(j == 0)
    def _():
        buf[0:pad] = prev_ref[0]

    buf[pad:pad + tb] = a_ref[0]
    base = pad - (CONV_A_WIDTH - 1) * stride
    for r0 in range(0, tb, rows):
        acc = jnp.broadcast_to(b_ref[...], (rows, A_CH))
        for tap in range(CONV_A_WIDTH):
            acc = acc + w_ref[tap:tap + 1, :] * buf[pl.ds(base + tap * stride + r0, rows), :]
        mu = jnp.mean(acc, axis=-1, keepdims=True)
        d = acc - mu
        var = jnp.mean(d * d, axis=-1, keepdims=True)
        y = d * lax.rsqrt(var + EPS) * g_ref[...] + beta_ref[...]
        o_ref[0, r0:r0 + rows, :] = (y * jax.nn.sigmoid(y)).astype(BF)
    if carry:
        buf[0:pad] = buf[tb:tb + pad]


def _conv_tail(a, prev, w, b, g, beta, tb, stride, rows):
    grp, t, _ = a.shape
    pad = prev.shape[1]
    wp = jnp.concatenate([w, jnp.zeros((1, A_CH), w.dtype)], axis=0)
    body = functools.partial(_conv_tail_body, tb=tb, pad=pad, stride=stride, rows=rows, carry=t > tb)
    return pl.pallas_call(
        body,
        grid=(grp, t // tb),
        in_specs=[pl.BlockSpec((1, tb, A_CH), lambda i, j: (i, j, 0)),
                  pl.BlockSpec((1, pad, A_CH), lambda i, j: (i, 0, 0)),
                  _full(wp.shape), _full((1, A_CH)), _full((1, A_CH)), _full((1, A_CH))],
        out_specs=pl.BlockSpec((1, tb, A_CH), lambda i, j: (i, j, 0)),
        out_shape=jax.ShapeDtypeStruct((grp, t, A_CH), BF),
        scratch_shapes=[pltpu.VMEM((pad + tb, A_CH), F32)],
        compiler_params=_params(("parallel", "arbitrary")),
        name="conv_tail",
    )(a, prev, wp, b[None, :], g[None, :], beta[None, :])


def _sort_keys(score):
    score = jnp.where(score == 0.0, 0.0, score)
    bits = pltpu.bitcast(score, jnp.int32)
    return jnp.where(bits < 0, bits ^ jnp.int32(0x7FFFFFFF), bits)


def _kth_largest(key_ref, width, nsel):
    rows = key_ref.shape[0]
    int_min = jnp.int32(-2 ** 31)

    def step(it, thr):
        cand = thr + lax.shift_left(jnp.int32(1), 31 - it)
        cnt = jnp.sum(jnp.where(key_ref[:, 0:width] >= cand, 1.0, 0.0), axis=-1, keepdims=True)
        return jnp.where(cnt >= nsel, cand, thr)

    return lax.fori_loop(0, 32, step, jnp.full((rows, 1), int_min, jnp.int32))


def _select_bias(key_ref, bias_ref, valid, tri_ref, width, nsel, chunk, row_ok=None):
    thr = _kth_largest(key_ref, width, nsel)
    key = key_ref[:, 0:width]
    cnt_gt = jnp.sum(jnp.where(key > thr, 1.0, 0.0), axis=-1, keepdims=True)
    cnt_eq = jnp.sum(jnp.where(key == thr, 1.0, 0.0), axis=-1, keepdims=True)
    need = nsel - cnt_gt
    bias_ref[:, 0:width] = jnp.where(jnp.logical_and(key >= thr, valid), 0.0, NEG)

    surplus = cnt_eq - need
    if row_ok is not None:
        surplus = jnp.where(row_ok, surplus, 0.0)

    @pl.when(jnp.max(surplus) > 0.0)
    def _():
        off = jnp.zeros_like(need)
        for c0 in range(0, width, chunk):
            kc = key_ref[:, c0:c0 + chunk]
            eq = kc == thr
            eqf = jnp.where(eq, 1.0, 0.0)
            rank = _dot(eqf.astype(BF), tri_ref[...]) + off
            take = jnp.logical_or(kc > thr, jnp.logical_and(eq, rank < need))
            bias_ref[:, c0:c0 + chunk] = jnp.where(jnp.logical_and(take, valid[:, c0:c0 + chunk]), 0.0, NEG)
            off = off + jnp.sum(eqf, axis=-1, keepdims=True)


def _strict_upper(n):
    return (jnp.arange(n)[:, None] < jnp.arange(n)[None, :]).astype(BF)


def _stack_heads(x2, low, keep_low):
    zero = jnp.zeros_like(x2)
    return jnp.where(low, x2, zero) if keep_low else jnp.where(low, zero, x2)


def _dsa_prompt_block(q_ref, qih_ref, qil_ref, wi_ref, k_ref, v_ref, kih_ref, kil_ref, tri_ref, o_ref,
                      key_ref, bias_ref,
                      *, tq, sx, nsel):
    i = pl.program_id(1)
    low = _low_half(tq)
    qpos = i * tq + lax.broadcasted_iota(jnp.int32, (tq, sx), 0)
    kpos = lax.broadcasted_iota(jnp.int32, (tq, sx), 1)
    valid = kpos <= qpos

    def stack_idx(qi):
        return jnp.concatenate([_stack_heads(qi[:, (h // 2) * LANES:(h // 2 + 1) * LANES], low, h % 2 == 0)
                                for h in range(H_IDX)], axis=0)

    hi, lw = stack_idx(qih_ref[0]), stack_idx(qil_ref[0])
    logits = (_dot_nt(hi, kih_ref[0, 0:sx, :]) + _dot_nt(hi, kil_ref[0, 0:sx, :])
              + _dot_nt(lw, kih_ref[0, 0:sx, :]))
    wi = wi_ref[0]
    score = jnp.zeros((tq, sx), F32)
    for h in range(H_IDX):
        score = score + wi[:, h:h + 1] * jnp.maximum(logits[h * tq:(h + 1) * tq], 0.0)
    score = jnp.where(valid, score, -jnp.inf)
    key_ref[:, 0:sx] = _sort_keys(score)
    _select_bias(key_ref, bias_ref, valid, tri_ref, sx, nsel, LANES)

    q = q_ref[0]
    bias = bias_ref[:, 0:sx]
    bias4 = jnp.concatenate([bias] * 4, axis=0)
    outs = []
    for g in range(HKV_B):
        stack = jnp.concatenate([_stack_heads(q[:, p * LANES:(p + 1) * LANES], low, g == 0)
                                 for p in range(4)], axis=0)
        s = _dot_nt(stack, k_ref[0, 0:sx, :]) + bias4
        e = jnp.exp(s - jnp.max(s, axis=-1, keepdims=True))
        l = jnp.sum(e, axis=-1, keepdims=True)
        outs.append(_dot(e.astype(BF), v_ref[0, 0:sx, :]) / l)
    for p in range(4):
        o_ref[0, :, p * LANES:(p + 1) * LANES] = jnp.where(
            low, outs[0][p * tq:(p + 1) * tq], outs[1][p * tq:(p + 1) * tq]).astype(BF)


def _causal_extents(n_blocks, tq, s):
    nb = 4 if n_blocks % 4 == 0 else 1
    per = n_blocks // nb
    return per, [min(s, (c + 1) * per * tq) for c in range(nb)]


def _dsa_prompt_body(*refs, tq, s, nsel):
    per, extents = _causal_extents(s // tq, tq, s)
    i = pl.program_id(1)
    for c, sx in enumerate(extents):
        pl.when(i // per == c)(functools.partial(_dsa_prompt_block, *refs, tq=tq, sx=sx, nsel=nsel))


def _dsa_prompt(q, qih, qil, wi, kb, vb, kih, kil, tq):
    n, s, _ = q.shape
    nsel = min(TOPK_MAX, s // 4)
    tri = _strict_upper(LANES)
    qblk = lambda width: pl.BlockSpec((1, tq, width), lambda i, j: (i, j, 0))
    sblk = lambda width: pl.BlockSpec((1, s, width), lambda i, j: (i, 0, 0))
    return pl.pallas_call(
        functools.partial(_dsa_prompt_body, tq=tq, s=s, nsel=nsel),
        grid=(n, s // tq),
        in_specs=[qblk(512), qblk(256), qblk(256), qblk(LANES), sblk(LANES), sblk(LANES), sblk(LANES),
                  sblk(LANES), _full(tri.shape)],
        out_specs=qblk(512),
        out_shape=jax.ShapeDtypeStruct((n, s, 512), BF),
        scratch_shapes=[pltpu.VMEM((tq, s), jnp.int32), pltpu.VMEM((tq, s), F32)],
        compiler_params=_params(("parallel", "parallel")),
        name="dsa_prompt",
    )(q, qih, qil, wi, kb, vb, kih, kil, tri)


def _out_proj_body(x_ref, a_ref, b_ref, wa_ref, wb_ref, o_ref):
    o_ref[0] = x_ref[0] + _dot(a_ref[0], wa_ref[...]) + _dot(b_ref[0], wb_ref[...])


def _out_proj(x, a, b, wa, wb, tb):
    grp, t, _ = x.shape
    blk = lambda width: pl.BlockSpec((1, tb, width), lambda i, j: (i, j, 0))
    return pl.pallas_call(
        _out_proj_body,
        grid=(grp, t // tb),
        in_specs=[blk(D_MODEL), blk(a.shape[-1]), blk(b.shape[-1]), _full(wa.shape), _full(wb.shape)],
        out_specs=blk(D_MODEL),
        out_shape=jax.ShapeDtypeStruct(x.shape, F32),
        compiler_params=_params(("parallel", "parallel")),
        name="out_proj",
    )(x, a, b, wa, wb)


FFN_TILE = 256


def _ffn_body(x_ref, g_ref, wu_ref, cw_ref, cb_ref, wd_ref, prev_ref, o_ref, tail_ref,
              hbuf, carry, ubuf, acc, *, tb, pad, stride):
    j = pl.program_id(1)
    hbuf[...] = _rms(x_ref[0], g_ref[...]).astype(BF)

    @pl.when(j == 0)
    def _():
        carry[...] = prev_ref[0]

    acc[...] = x_ref[0]
    for f in range(D_FF // FFN_TILE):
        halves = []
        for part in range(2):
            c0 = part * D_FF + f * FFN_TILE
            u = _dot(hbuf[...], wu_ref[:, c0:c0 + FFN_TILE])
            ubuf[0:pad] = carry[:, c0:c0 + FFN_TILE]
            ubuf[pad:pad + tb] = u
            y = (cw_ref[0:1, c0:c0 + FFN_TILE] * ubuf[pl.ds(pad - 2 * stride, tb), :]
                 + cw_ref[1:2, c0:c0 + FFN_TILE] * ubuf[pl.ds(pad - stride, tb), :]
                 + cw_ref[2:3, c0:c0 + FFN_TILE] * u + cb_ref[:, c0:c0 + FFN_TILE])
            carry[:, c0:c0 + FFN_TILE] = ubuf[tb:tb + pad]
            halves.append(y)
        act = (halves[1] * jax.nn.sigmoid(halves[1]) * halves[0]).astype(BF)
        acc[...] += _dot(act, wd_ref[f * FFN_TILE:(f + 1) * FFN_TILE, :])
    o_ref[0] = acc[...]

    @pl.when(j == pl.num_programs(1) - 1)
    def _():
        tail_ref[0] = carry[...]


def _ffn(x, g, wu, cw, cb, wd, prev, tb, stride):
    grp, t, _ = x.shape
    pad = prev.shape[1]
    cwp = jnp.concatenate([cw, jnp.zeros((SUBLANES - FFN_CONV_WIDTH, 2 * D_FF), cw.dtype)], axis=0)
    blk = pl.BlockSpec((1, tb, D_MODEL), lambda i, j: (i, j, 0))
    pblk = pl.BlockSpec((1, pad, 2 * D_FF), lambda i, j: (i, 0, 0))
    once = lambda shape: pl.BlockSpec(shape, lambda i, j: (0,) * len(shape), pipeline_mode=pl.Buffered(1))
    return pl.pallas_call(
        functools.partial(_ffn_body, tb=tb, pad=pad, stride=stride),
        grid=(grp, t // tb),
        in_specs=[blk, _full((1, D_MODEL)), once(wu.shape), _full(cwp.shape), _full((1, 2 * D_FF)),
                  once(wd.shape), pblk],
        out_specs=[blk, pblk],
        out_shape=[jax.ShapeDtypeStruct(x.shape, F32), jax.ShapeDtypeStruct(prev.shape, F32)],
        scratch_shapes=[pltpu.VMEM((tb, D_MODEL), BF), pltpu.VMEM((pad, 2 * D_FF), F32),
                        pltpu.VMEM((pad + tb, FFN_TILE), F32), pltpu.VMEM((tb, D_MODEL), F32)],
        compiler_params=_params(("parallel", "arbitrary")),
        name="conv_ffn",
    )(x, g[None, :], wu, cwp, cb[None, :], wd, prev)


def _head_norm128(x, gain, ind_ref, indt_ref, dim):
    ssq = _dot2(x * x, ind_ref[...])
    r = lax.rsqrt(ssq / dim + EPS)
    return x * _dot2(r, indt_ref[...]) * gain


def _odd_proj_body(x_ref, g_ref, w_ref, qan_ref, wqb_ref, kvn_ref, qn_ref, kn_ref, ind_ref, indt_ref,
                   c_ref, sa_ref, sb_ref, glg_ref, glb_ref, wuk_ref, wuv_ref,
                   ckv_ref, kpe_ref, u_ref, v_ref, q_ref, k_ref, vv_ref):
    h = _rms(x_ref[0], g_ref[...]).astype(BF)
    z = _dot(h, w_ref[...])
    c, sa, sb = c_ref[...], sa_ref[...], sb_ref[...]
    half = ROPE_C // 2

    qa = _rms(z[:, O_QA:O_CKV], qan_ref[...]).astype(BF)
    q = _head_norm128(_dot(qa, wqb_ref[...]), qn_ref[...], ind_ref, indt_ref, DQK_C)
    for hd in range(HC):
        sl = slice(hd * LANES, (hd + 1) * LANES)
        q_ref[0, :, sl] = (_rope128(q[:, sl], c, sa, sb, half) * MLA_SCALE).astype(BF)

    ckv = _rms(z[:, O_CKV:O_KPE], kvn_ref[...])
    ckv_ref[0] = ckv
    cb = ckv.astype(BF)
    kpe = z[:, O_KPE:O_U]
    kpe_ref[0] = kpe[:, :ROPE_C]
    kfull = _dot(cb, wuk_ref[...]) + jnp.concatenate([kpe] * HC, axis=1)
    k = _head_norm128(kfull, kn_ref[...], ind_ref, indt_ref, DQK_C)
    for hd in range(HC):
        sl = slice(hd * LANES, (hd + 1) * LANES)
        k_ref[0, :, sl] = _rope128(k[:, sl], c, sa, sb, half).astype(BF)
    vv_ref[0] = _dot(cb, wuv_ref[...]).astype(BF)

    zz = jax.nn.gelu(z[:, O_U:O_END])
    u_ref[0] = zz[:, :D_CH]
    vz = zz[:, D_CH:]
    mu = jnp.mean(vz, axis=-1, keepdims=True)
    d = vz - mu
    var = jnp.mean(d * d, axis=-1, keepdims=True)
    v_ref[0] = d * lax.rsqrt(var + EPS) * glg_ref[...] + glb_ref[...]


def _pad_heads(w, lead):
    z32 = jnp.zeros(lead + (HC, ROPE_C), w.dtype)
    return jnp.concatenate([z32, w, z32], axis=-1).reshape(lead + (HC * LANES,))


def _odd_weights(w_in, w_qb, w_uk, w_uv, q_norm, k_norm):
    d = w_in.shape[0]
    cuts = [Q_LORA, Q_LORA + KV_LORA, Q_LORA + KV_LORA + ROPE_C]
    pad = jnp.zeros((d, LANES - ROPE_C), w_in.dtype)
    w = jnp.concatenate([w_in[:, :cuts[2]], pad, w_in[:, cuts[2]:]], axis=1).astype(BF)
    qb = w_qb.reshape(Q_LORA, HC, DQK_C)
    qb = jnp.concatenate([qb, jnp.zeros((Q_LORA, HC, LANES - DQK_C), w_qb.dtype)], axis=-1)
    wqb = qb.reshape(Q_LORA, HC * LANES).astype(BF)
    wuk_pad = _pad_heads(w_uk, (KV_LORA,)).astype(BF)
    wuk = w_uk.reshape(KV_LORA, HC * NOPE_C).astype(BF)
    wuv = w_uv.reshape(KV_LORA, HC * VD_C).astype(BF)
    gain = lambda g: jnp.tile(jnp.concatenate([g, jnp.zeros((LANES - DQK_C,), g.dtype)]), HC)[None, :]
    return w, wqb, wuk_pad, wuk, wuv, gain(q_norm), gain(k_norm)


def _odd_proj(x, g, w, wqb, wuk_pad, wuv, qan, kvn, qn, kn, glg, glb, tables, tb):
    grp, t, _ = x.shape
    ind, indt = _indicator(HC * LANES, LANES)
    blk = lambda width: pl.BlockSpec((1, tb, width), lambda i, j: (i, j, 0))
    tab = pl.BlockSpec((tb, LANES), lambda i, j: (j, 0))
    sds = lambda width, dt: jax.ShapeDtypeStruct((grp, t, width), dt)
    row = lambda v: v[None, :]
    return pl.pallas_call(
        _odd_proj_body,
        grid=(grp, t // tb),
        in_specs=[blk(D_MODEL), _full((1, D_MODEL)), _full(w.shape), _full((1, Q_LORA)), _full(wqb.shape),
                  _full((1, KV_LORA)), _full(qn.shape), _full(kn.shape), _full(ind.shape), _full(indt.shape),
                  tab, tab, tab, _full((1, D_CH)), _full((1, D_CH)), _full(wuk_pad.shape), _full(wuv.shape)],
        out_specs=[blk(KV_LORA), blk(ROPE_C), blk(D_CH), blk(D_CH), blk(HC * LANES), blk(HC * LANES),
                   blk(HC * VD_C)],
        out_shape=[sds(KV_LORA, F32), sds(ROPE_C, F32), sds(D_CH, F32), sds(D_CH, F32),
                   sds(HC * LANES, BF), sds(HC * LANES, BF), sds(HC * VD_C, BF)],
        compiler_params=_params(("parallel", "parallel")),
        name="odd_proj",
    )(x, row(g), w, row(qan), wqb, row(kvn), qn, kn, ind, indt, *tables, row(glg), row(glb), wuk_pad, wuv)


def _gmlp_prompt_body(u_ref, v_ref, w_ref, b_ref, o_ref, *, tb):
    rows = D_GROUPS * CHUNK
    t_in = lax.broadcasted_iota(jnp.int32, (rows, CHUNK), 0) % CHUNK
    s_in = lax.broadcasted_iota(jnp.int32, (rows, CHUNK), 1)
    w = jnp.where(s_in <= t_in, w_ref[...], 0.0).astype(BF)
    grp = lax.broadcasted_iota(jnp.int32, (CHUNK, D_CH), 1) // (D_CH // D_GROUPS)
    for c0 in range(0, tb, CHUNK):
        y = _dot(w, v_ref[0, c0:c0 + CHUNK, :].astype(BF))
        mix = b_ref[...]
        for gi in range(D_GROUPS):
            mix = mix + jnp.where(grp == gi, y[gi * CHUNK:(gi + 1) * CHUNK], 0.0)
        o_ref[0, c0:c0 + CHUNK, :] = (u_ref[0, c0:c0 + CHUNK, :] * mix).astype(BF)


def _gmlp_prompt(u, v, w_s, b_s, tb):
    n, s, _ = u.shape
    w = w_s.reshape(D_GROUPS * CHUNK, CHUNK)
    b = jnp.repeat(b_s.T, D_CH // D_GROUPS, axis=1)
    blk = pl.BlockSpec((1, tb, D_CH), lambda i, j: (i, j, 0))
    return pl.pallas_call(
        functools.partial(_gmlp_prompt_body, tb=tb),
        grid=(n, s // tb),
        in_specs=[blk, blk, _full(w.shape), _full(b.shape)],
        out_specs=blk,
        out_shape=jax.ShapeDtypeStruct(u.shape, BF),
        compiler_params=_params(("parallel", "parallel")),
        name="gmlp_prompt",
    )(u, v, w, b)


def _gmlp_sample_body(u_ref, v_ref, w_ref, b_ref, o_ref, *, nb, nt):
    for t in range(nt):
        mix = jnp.broadcast_to(b_ref[t:t + 1, :], (nb, D_CH))
        for s in range(t + 1):
            mix = mix + w_ref[t * nt + s:t * nt + s + 1, :] * v_ref[s * nb:(s + 1) * nb, :]
        o_ref[t * nb:(t + 1) * nb, :] = (u_ref[t * nb:(t + 1) * nb, :] * mix).astype(BF)


def _gmlp_sample(u, v, w_s, b_s, nb, nt):
    lane = D_CH // D_GROUPS
    w = jnp.repeat(jnp.transpose(w_s[:, :nt, :nt], (1, 2, 0)).reshape(nt * nt, D_GROUPS), lane, axis=1)
    b = jnp.repeat(b_s.T[:nt], lane, axis=1)
    return pl.pallas_call(
        functools.partial(_gmlp_sample_body, nb=nb, nt=nt),
        out_shape=jax.ShapeDtypeStruct(u.shape, BF),
        name="gmlp_sample",
    )(u, v, w, b)


def _mla_prompt_block(q_ref, k_ref, v_ref, o_ref, *, tq, sx):
    i = pl.program_id(1)
    low = _low_half(tq)
    qpos = i * tq + lax.broadcasted_iota(jnp.int32, (tq, sx), 0)
    kpos = lax.broadcasted_iota(jnp.int32, (tq, sx), 1)
    bias = jnp.where(kpos <= qpos, 0.0, NEG)
    for p in range(HC // 2):
        outs = []
        for hd in (2 * p, 2 * p + 1):
            sl = slice(hd * LANES, (hd + 1) * LANES)
            s = _dot_nt(q_ref[0, :, sl], k_ref[0, 0:sx, sl]) + bias
            e = jnp.exp(s - jnp.max(s, axis=-1, keepdims=True))
            l = jnp.sum(e, axis=-1, keepdims=True)
            outs.append(_dot(e.astype(BF), v_ref[0, 0:sx, p * LANES:(p + 1) * LANES]) / l)
        o_ref[0, :, p * LANES:(p + 1) * LANES] = jnp.where(low, outs[0], outs[1]).astype(BF)


def _mla_prompt_body(*refs, tq, s):
    per, extents = _causal_extents(s // tq, tq, s)
    i = pl.program_id(1)
    for c, sx in enumerate(extents):
        pl.when(i // per == c)(functools.partial(_mla_prompt_block, *refs, tq=tq, sx=sx))


def _mla_prompt(q, k, v, tq):
    n, s, _ = q.shape
    return pl.pallas_call(
        functools.partial(_mla_prompt_body, tq=tq, s=s),
        grid=(n, s // tq),
        in_specs=[pl.BlockSpec((1, tq, HC * LANES), lambda i, j: (i, j, 0)),
                  pl.BlockSpec((1, s, HC * LANES), lambda i, j: (i, 0, 0)),
                  pl.BlockSpec((1, s, HC * VD_C), lambda i, j: (i, 0, 0))],
        out_specs=pl.BlockSpec((1, tq, HC * VD_C), lambda i, j: (i, j, 0)),
        out_shape=jax.ShapeDtypeStruct((n, s, HC * VD_C), BF),
        compiler_params=_params(("parallel", "parallel")),
        name="mla_prompt",
    )(q, k, v)


PAGES_PER_STEP = 16


def _page_specs(cache_shape, layer, n_tables=1):
    _, _, rows, width = cache_shape

    def spec(slot):
        return pl.BlockSpec((None, None, rows, width),
                            lambda s, j, pt: (layer, pt[s, j * PAGES_PER_STEP + slot], 0, 0))

    return [spec(slot) for slot in range(PAGES_PER_STEP)]


def _cat_pages(refs):
    return jnp.concatenate([r[...] for r in refs], axis=0)


def _idx_scores(qh, ql, wi, kh, kl):
    logits = _dot_nt(qh, kh) + _dot_nt(qh, kl) + _dot_nt(ql, kh)
    score = jnp.zeros((SUBLANES, logits.shape[1]), F32)
    for h in range(H_IDX):
        sl = slice(h * SUBLANES, (h + 1) * SUBLANES)
        score = score + wi[sl, 0:1] * jnp.maximum(logits[sl], 0.0)
    return score


def _dsa_sample_score_body(pt_ref, qh_ref, ql_ref, wi_ref, *rest):
    del pt_ref
    pages, o_ref = rest[:PAGES_PER_STEP], rest[PAGES_PER_STEP]
    kh, kl = _split2(_cat_pages(pages))
    o_ref[0] = _idx_scores(qh_ref[0], ql_ref[0], wi_ref[0], kh, kl)


def _dsa_sample_scores(page_table, qh, ql, wi, cache_ki, layer):
    n, n_pages = page_table.shape
    steps = n_pages // PAGES_PER_STEP
    span = PAGES_PER_STEP * PAGE_SIZE
    per_seq = lambda shape: pl.BlockSpec((1,) + shape, lambda s, j, pt: (s, 0, 0))
    grid_spec = pltpu.PrefetchScalarGridSpec(
        num_scalar_prefetch=1,
        grid=(n, steps),
        in_specs=[per_seq((4 * SUBLANES, D_IDX)), per_seq((4 * SUBLANES, D_IDX)), per_seq((4 * SUBLANES, LANES))]
        + _page_specs(cache_ki.shape, layer),
        out_specs=pl.BlockSpec((1, SUBLANES, span), lambda s, j, pt: (s, 0, j)),
    )
    return pl.pallas_call(
        _dsa_sample_score_body,
        grid_spec=grid_spec,
        out_shape=jax.ShapeDtypeStruct((n, SUBLANES, n_pages * PAGE_SIZE), F32),
        compiler_params=_params(("parallel", "arbitrary")),
        name="dsa_sample_scores",
    )(page_table, qh, ql, wi, *([cache_ki] * PAGES_PER_STEP))


def _online_update(m_ref, l_ref, acc_ref, s, v):
    m_old = m_ref[...]
    m_new = jnp.maximum(m_old, jnp.max(s, axis=-1, keepdims=True))
    alpha = jnp.exp(m_old - m_new)
    e = jnp.exp(s - m_new)
    l_ref[...] = alpha * l_ref[...] + jnp.sum(e, axis=-1, keepdims=True)
    acc_ref[...] = alpha * acc_ref[...] + _dot(e.astype(BF), v)
    m_ref[...] = m_new


def _dsa_sample_attend_body(pt_ref, score_ref, qh_ref, ql_ref, wi_ref, knh_ref, knl_ref, q_ref, kn_ref, vn_ref,
                            tri_ref, *rest, past, nsel, nt):
    del pt_ref
    kpages = rest[:PAGES_PER_STEP]
    vpages = rest[PAGES_PER_STEP:2 * PAGES_PER_STEP]
    o_ref, key_ref, bias_ref, q64_ref, m_ref, l_ref, acc_ref = rest[2 * PAGES_PER_STEP:]
    j = pl.program_id(1)
    span = PAGES_PER_STEP * PAGE_SIZE
    width = past + PAGE_SIZE
    rows8 = SUBLANES

    @pl.when(j == 0)
    def _():
        new = _idx_scores(qh_ref[0], ql_ref[0], wi_ref[0], knh_ref[0], knl_ref[0])
        row = lax.broadcasted_iota(jnp.int32, (rows8, width), 0)
        kpos = lax.broadcasted_iota(jnp.int32, (rows8, width), 1)
        valid = kpos <= past + row
        key_ref[:, 0:past] = _sort_keys(score_ref[0])
        key_ref[:, past:width] = _sort_keys(jnp.where(valid[:, past:width], new, -jnp.inf))
        row_ok = lax.broadcasted_iota(jnp.int32, (rows8, 1), 0) < nt
        _select_bias(key_ref, bias_ref, valid, tri_ref, width, nsel, LANES, row_ok)
        low = _low_half(rows8)
        per = HB // HKV_B
        q8 = q_ref[0].astype(F32)
        q64_ref[...] = jnp.concatenate(
            [_stack_heads(q8[:, (h % per) * LANES:(h % per + 1) * LANES], low, h < per) for h in range(HB)],
            axis=0).astype(BF)
        m_ref[...] = jnp.full(m_ref.shape, NEG, F32)
        l_ref[...] = jnp.zeros(l_ref.shape, F32)
        acc_ref[...] = jnp.zeros(acc_ref.shape, F32)

    def fold(bias8, k, v):
        s = _dot_nt(q64_ref[...], k) + jnp.concatenate([bias8] * HB, axis=0)
        _online_update(m_ref, l_ref, acc_ref, s, v)

    off = pl.multiple_of(j * span, span)
    fold(bias_ref[:, pl.ds(off, span)], _cat_pages(kpages).astype(BF), _cat_pages(vpages).astype(BF))

    @pl.when(j == pl.num_programs(1) - 1)
    def _():
        fold(bias_ref[:, past:width], kn_ref[0], vn_ref[0])
        out = acc_ref[...] / l_ref[...]
        low = _low_half(rows8)
        per = HB // HKV_B
        for p in range(per):
            o_ref[0, :, p * LANES:(p + 1) * LANES] = jnp.where(
                low, out[p * rows8:(p + 1) * rows8], out[(per + p) * rows8:(per + p + 1) * rows8]).astype(BF)


def _dsa_sample_attend(page_table, score, qh, ql, wi, knh, knl, q8, kn, vn, cache_k, cache_v, layer, nt):
    n, n_pages = page_table.shape
    past = n_pages * PAGE_SIZE
    nsel = min(TOPK_MAX, (past + nt) // 4)
    tri = _strict_upper(LANES)
    width = past + PAGE_SIZE
    per_seq = lambda shape: pl.BlockSpec((1,) + shape, lambda s, j, pt: (s, 0, 0))
    grid_spec = pltpu.PrefetchScalarGridSpec(
        num_scalar_prefetch=1,
        grid=(n, n_pages // PAGES_PER_STEP),
        in_specs=[per_seq((SUBLANES, past)), per_seq((4 * SUBLANES, D_IDX)), per_seq((4 * SUBLANES, D_IDX)),
                  per_seq((4 * SUBLANES, LANES)), per_seq((PAGE_SIZE, D_IDX)), per_seq((PAGE_SIZE, D_IDX)),
                  per_seq((SUBLANES, HB * DH_B)), per_seq((PAGE_SIZE, LANES)), per_seq((PAGE_SIZE, LANES)),
                  pl.BlockSpec(tri.shape, lambda s, j, pt: (0, 0))]
        + _page_specs(cache_k.shape, layer) + _page_specs(cache_v.shape, layer),
        out_specs=per_seq((SUBLANES, HB * DH_B)),
        scratch_shapes=[pltpu.VMEM((SUBLANES, width), jnp.int32), pltpu.VMEM((SUBLANES, width), F32),
                        pltpu.VMEM((HB * SUBLANES, LANES), BF), pltpu.VMEM((HB * SUBLANES, 1), F32),
                        pltpu.VMEM((HB * SUBLANES, 1), F32), pltpu.VMEM((HB * SUBLANES, LANES), F32)],
    )
    return pl.pallas_call(
        functools.partial(_dsa_sample_attend_body, past=past, nsel=nsel, nt=nt),
        grid_spec=grid_spec,
        out_shape=jax.ShapeDtypeStruct((n, SUBLANES, HB * DH_B), BF),
        compiler_params=_params(("parallel", "arbitrary")),
        name="dsa_sample_attend",
    )(page_table, score, qh, ql, wi, knh, knl, q8, kn, vn, tri,
      *([cache_k] * PAGES_PER_STEP), *([cache_v] * PAGES_PER_STEP))


def _mla_sample_body(pt_ref, q_ref, kn_ref, vn_ref, wukp_ref, wuk_ref, wuv_ref, gk_ref, ind_ref, ct_ref, st_ref,
                     *rest, nt):
    del pt_ref
    cpages = rest[:PAGES_PER_STEP]
    ppages = rest[PAGES_PER_STEP:2 * PAGES_PER_STEP]
    o_ref, qbd_ref, qabs_ref, qpe_ref, qsw_ref, m_ref, l_ref, acc_ref = rest[2 * PAGES_PER_STEP:]
    j = pl.program_id(1)
    rows = HC * SUBLANES

    @pl.when(j == 0)
    def _():
        q8 = q_ref[0].astype(F32)
        rhead = lax.broadcasted_iota(jnp.int32, (rows, HC * LANES), 0) // SUBLANES
        lhead = lax.broadcasted_iota(jnp.int32, (rows, HC * LANES), 1) // LANES
        qbd = jnp.where(rhead == lhead, jnp.concatenate([q8] * HC, axis=0), 0.0)
        qbd_ref[...] = qbd.astype(BF)
        qabs_ref[...] = _dot_nt((qbd * gk_ref[...]).astype(BF), wukp_ref[...]).astype(BF)
        folded = qbd[:, 0:LANES]
        for hd in range(1, HC):
            folded = folded + qbd[:, hd * LANES:(hd + 1) * LANES]
        lane = lax.broadcasted_iota(jnp.int32, (rows, LANES), 1)
        half = ROPE_C // 2
        swapped = jnp.where(lane < half, pltpu.roll(folded, LANES - half, 1),
                            jnp.where(lane < ROPE_C, -pltpu.roll(folded, half, 1), 0.0))
        gpe = gk_ref[:, 0:LANES]
        qpe_ref[...] = (folded * gpe)[:, 0:ROPE_C].astype(BF)
        qsw_ref[...] = (swapped * gpe)[:, 0:ROPE_C].astype(BF)
        m_ref[...] = jnp.full(m_ref.shape, NEG, F32)
        l_ref[...] = jnp.zeros(l_ref.shape, F32)
        acc_ref[...] = jnp.zeros(acc_ref.shape, F32)

    cb = _cat_pages(cpages).astype(BF)
    kpe = _cat_pages(ppages)
    kn = _dot(cb, wuk_ref[...])
    ssq = _dot2_nt(ind_ref[...], kn * kn) + _dot2_nt(jnp.ones((rows, ROPE_C), BF), kpe * kpe)
    r = lax.rsqrt(ssq / DQK_C + EPS)
    s = r * (_dot_nt(qabs_ref[...], cb)
             + _dot_nt(qpe_ref[...], (kpe * ct_ref[...]).astype(BF))
             + _dot_nt(qsw_ref[...], (kpe * st_ref[...]).astype(BF)))
    _online_update(m_ref, l_ref, acc_ref, s, cb)

    @pl.when(j == pl.num_programs(1) - 1)
    def _():
        s_new = _dot_nt(qbd_ref[...], kn_ref[0])
        t_row = lax.broadcasted_iota(jnp.int32, s_new.shape, 0) % SUBLANES
        col = lax.broadcasted_iota(jnp.int32, s_new.shape, 1)
        s_new = jnp.where(jnp.logical_and(col <= t_row, col < nt), s_new, NEG)
        m_old = m_ref[...]
        m_new = jnp.maximum(m_old, jnp.max(s_new, axis=-1, keepdims=True))
        alpha = jnp.exp(m_old - m_new)
        e = jnp.exp(s_new - m_new)
        l = alpha * l_ref[...] + jnp.sum(e, axis=-1, keepdims=True)
        out = (_dot((alpha * acc_ref[...]).astype(BF), wuv_ref[...]) + _dot(e.astype(BF), vn_ref[0])) / l
        lhead = lax.broadcasted_iota(jnp.int32, (SUBLANES, HC * VD_C), 1) // VD_C
        res = jnp.zeros((SUBLANES, HC * VD_C), F32)
        for hd in range(HC):
            res = res + jnp.where(lhead == hd, out[hd * SUBLANES:(hd + 1) * SUBLANES], 0.0)
        o_ref[0] = res.astype(BF)


def _mla_sample(page_table, q8, kn, vn, wuk_pad, wuk, wuv, gk, ctab, stab, cache_ckv, cache_kpe, layer, nt):
    n, n_pages = page_table.shape
    span = PAGES_PER_STEP * PAGE_SIZE
    rows = HC * SUBLANES
    ind = (jnp.arange(rows)[:, None] // SUBLANES == jnp.arange(HC * NOPE_C)[None, :] // NOPE_C).astype(BF)
    per_seq = lambda shape: pl.BlockSpec((1,) + shape, lambda s, j, pt: (s, 0, 0))
    const = lambda a: pl.BlockSpec(a.shape, lambda s, j, pt: (0,) * a.ndim)
    tab = pl.BlockSpec((span, ROPE_C), lambda s, j, pt: (j, 0))
    grid_spec = pltpu.PrefetchScalarGridSpec(
        num_scalar_prefetch=1,
        grid=(n, n_pages // PAGES_PER_STEP),
        in_specs=[per_seq((SUBLANES, HC * LANES)), per_seq((PAGE_SIZE, HC * LANES)), per_seq((PAGE_SIZE, HC * VD_C)),
                  const(wuk_pad), const(wuk), const(wuv), const(gk), const(ind), tab, tab]
        + _page_specs(cache_ckv.shape, layer) + _page_specs(cache_kpe.shape, layer),
        out_specs=per_seq((SUBLANES, HC * VD_C)),
        scratch_shapes=[pltpu.VMEM((rows, HC * LANES), BF), pltpu.VMEM((rows, KV_LORA), BF),
                        pltpu.VMEM((rows, ROPE_C), BF), pltpu.VMEM((rows, ROPE_C), BF),
                        pltpu.VMEM((rows, 1), F32), pltpu.VMEM((rows, 1), F32), pltpu.VMEM((rows, KV_LORA), F32)],
    )
    return pl.pallas_call(
        functools.partial(_mla_sample_body, nt=nt),
        grid_spec=grid_spec,
        out_shape=jax.ShapeDtypeStruct((n, SUBLANES, HC * VD_C), BF),
        compiler_params=_params(("parallel", "arbitrary")),
        name="mla_sample",
    )(page_table, q8, kn, vn, wuk_pad, wuk, wuv, gk, ind, ctab, stab,
      *([cache_ckv] * PAGES_PER_STEP), *([cache_kpe] * PAGES_PER_STEP))


PROMPT_ROWS = 512
CONV_ROWS = 256
DSA_QBLOCK = 128
MLA_QBLOCK = 256


def kernel(x_prompt, x_sample, cache_dsa_k, cache_dsa_v, cache_dsa_kidx, state_conv_a, cache_mla_ckv, cache_mla_kpe, state_ffn_conv, page_table, norm_mix, norm_ffn, w_in_e, conv_a_w, conv_a_b, conv_a_ln_g, conv_a_ln_b, q_norm_b, k_norm_b, w_out_e, w_in_o, q_a_norm, w_qb, kv_a_norm, w_uk, w_uv, q_norm_c, k_norm_c, gmlp_ln_g, gmlp_ln_b, w_spatial, b_spatial, w_out_o, w_up, ffn_conv_w, ffn_conv_b, w_down):
    bsz, seq, _ = x_prompt.shape
    nb, nt, _ = x_sample.shape
    depth = norm_mix.shape[0]
    past = page_table.shape[1] * PAGE_SIZE
    srows = nt * nb
    pos_p = jnp.arange(seq)
    pos_s = past + jnp.repeat(jnp.arange(nt), nb)

    def to_rows(a):
        return jnp.transpose(a, (1, 0, 2)).reshape(1, a.shape[1] * nb, a.shape[2])

    def to_seq(a):
        return jnp.transpose(a.reshape(-1, nb, a.shape[-1]), (1, 0, 2))

    def pad_rows(a, rows):
        return jnp.pad(a, ((0, 0), (0, rows - a.shape[1]), (0, 0)))

    def idx_stack(a):
        w = a.shape[-1] // H_IDX
        a = jnp.transpose(to_seq(a).reshape(nb, nt, H_IDX, w), (0, 2, 1, 3))
        a = jnp.pad(a, ((0, 0), (0, 0), (0, SUBLANES - nt), (0, 0)))
        return a.reshape(nb, H_IDX * SUBLANES, w)

    yp = x_prompt
    ys = to_rows(x_sample)
    pk, pv, pki, pca, pckv, pkpe, pff = [], [], [], [], [], [], []
    sk, sv, ski, sca, sckv, skpe, sgv, sff = [], [], [], [], [], [], [], []

    for layer in range(depth):
        if layer % 2 == 0:
            e = layer // 2
            w, nrm = _even_weights(w_in_e[e], q_norm_b[e], k_norm_b[e])
            tabs_p = _rope_tables(pos_p, ROT_B, DH_B)
            tabs_s = _rope_tables(pos_s, ROT_B, DH_B)
            ap, qp, kp, vp, kip, wip, qihp, qilp, kbp, vbp, kihp, kilp = _even_proj(
                yp, norm_mix[layer], w, nrm, tabs_p, PROMPT_ROWS)
            a_s, qs, ks, vs, kis, wis, qihs, qils, kbs, vbs, kihs, kils = _even_proj(
                ys, norm_mix[layer], w, nrm, tabs_s, srows)

            hist = CONV_A_WIDTH - 1
            conv = (conv_a_w[e], conv_a_b[e], conv_a_ln_g[e], conv_a_ln_b[e])
            cp = _conv_tail(ap, jnp.zeros((bsz, 32, A_CH), F32), *conv, CONV_ROWS, 1, 32)
            cs = _conv_tail(a_s, to_rows(state_conv_a[e]), *conv, srows, nb, nb)
            pca.append(ap[:, seq - hist:, :])
            sca.append(jnp.concatenate([state_conv_a[e], to_seq(a_s)], axis=1)[:, nt:, :])

            bp = _dsa_prompt(qp, qihp, qilp, wip, kbp, vbp, kihp, kilp, DSA_QBLOCK)

            qh32, ql32 = idx_stack(qihs), idx_stack(qils)
            wi32 = jnp.broadcast_to(idx_stack(wis[..., :H_IDX]), (nb, H_IDX * SUBLANES, LANES))
            n_pool = cache_dsa_k.shape[1]
            scores = _dsa_sample_scores(page_table, qh32, ql32, wi32, cache_dsa_kidx, e)
            bs8 = _dsa_sample_attend(
                page_table, scores, qh32, ql32, wi32,
                pad_rows(to_seq(kihs[..., :D_IDX]), PAGE_SIZE), pad_rows(to_seq(kils[..., :D_IDX]), PAGE_SIZE),
                pad_rows(to_seq(qs), SUBLANES), pad_rows(to_seq(kbs), PAGE_SIZE), pad_rows(to_seq(vbs), PAGE_SIZE),
                cache_dsa_k.reshape(-1, n_pool, PAGE_SIZE, HKV_B * DH_B),
                cache_dsa_v.reshape(-1, n_pool, PAGE_SIZE, HKV_B * DH_B), e, nt)
            bs = to_rows(bs8[:, :nt])

            wa = w_out_e[e][:A_CH].astype(BF)
            wb = w_out_e[e][A_CH:][_DSA_HEAD_PERM].astype(BF)
            yp = _out_proj(yp, cp, bp, wa, wb, PROMPT_ROWS)
            ys = _out_proj(ys, cs, bs, wa, wb, srows)

            pk.append(kp.reshape(bsz, seq, HKV_B, DH_B))
            pv.append(vp.reshape(bsz, seq, HKV_B, DH_B))
            pki.append(kip)
            sk.append(to_seq(ks).reshape(nb, nt, HKV_B, DH_B))
            sv.append(to_seq(vs).reshape(nb, nt, HKV_B, DH_B))
            ski.append(to_seq(kis))
        else:
            o = layer // 2
            w, wqb, wukp, wuk, wuv, qn, kn = _odd_weights(w_in_o[o], w_qb[o], w_uk[o], w_uv[o],
                                                          q_norm_c[o], k_norm_c[o])
            tabs_p = _rope_tables(pos_p, ROPE_C, LANES)
            tabs_s = _rope_tables(pos_s, ROPE_C, LANES)
            rest = (w, wqb, wukp, wuv, q_a_norm[o], kv_a_norm[o], qn, kn, gmlp_ln_g[o], gmlp_ln_b[o])
            ckvp, kpep, up, vp, q_p, k_p, v_p = _odd_proj(yp, norm_mix[layer], *rest, tabs_p, PROMPT_ROWS)
            ckvs, kpes, us, vs, q_s, k_s, v_s = _odd_proj(ys, norm_mix[layer], *rest, tabs_s, srows)

            mp = _mla_prompt(q_p, k_p, v_p, MLA_QBLOCK)
            gp = _gmlp_prompt(up, vp, w_spatial[o], b_spatial[o], PROMPT_ROWS)
            gs = _gmlp_sample(us[0], vs[0], w_spatial[o], b_spatial[o], nb, nt)[None]

            half = ROPE_C // 2
            inv_freq = jnp.power(jnp.float32(ROPE_THETA), -jnp.arange(half, dtype=F32) * (2.0 / ROPE_C))
            ang = jnp.arange(past).astype(F32)[:, None] * inv_freq[None, :]
            ctab = jnp.tile(jnp.cos(ang), (1, 2))
            stab = jnp.tile(jnp.sin(ang), (1, 2))
            ms8 = _mla_sample(page_table, pad_rows(to_seq(q_s), SUBLANES), pad_rows(to_seq(k_s), PAGE_SIZE),
                              pad_rows(to_seq(v_s), PAGE_SIZE), wukp, wuk, wuv, kn, ctab, stab,
                              cache_mla_ckv, cache_mla_kpe, o, nt)
            ms = to_rows(ms8[:, :nt])

            wa = w_out_o[o][:HC * VD_C].astype(BF)
            wb = w_out_o[o][HC * VD_C:].astype(BF)
            yp = _out_proj(yp, mp, gp, wa, wb, PROMPT_ROWS)
            ys = _out_proj(ys, ms, gs, wa, wb, srows)

            pckv.append(ckvp)
            pkpe.append(kpep)
            sckv.append(to_seq(ckvs))
            skpe.append(to_seq(kpes))
            sgv.append(to_seq(vs))

        keep = FFN_CONV_WIDTH - 1
        ffn = (norm_ffn[layer], w_up[layer].astype(BF), ffn_conv_w[layer], ffn_conv_b[layer],
               w_down[layer].astype(BF))
        yp, tail_p = _ffn(yp, *ffn, jnp.zeros((bsz, SUBLANES, 2 * D_FF), F32), PROMPT_ROWS, 1)
        ys, tail_s = _ffn(ys, *ffn, to_rows(state_ffn_conv[layer]), srows, nb)
        pff.append(tail_p[:, SUBLANES - keep:, :])
        sff.append(to_seq(tail_s))

    return (yp, to_seq(ys),
            jnp.stack(pk), jnp.stack(pv), jnp.stack(pki), jnp.stack(pca),
            jnp.stack(pckv), jnp.stack(pkpe), jnp.stack(pff),
            jnp.stack(sk), jnp.stack(sv), jnp.stack(ski), jnp.stack(sca),
            jnp.stack(sckv), jnp.stack(skpe), jnp.stack(sgv), jnp.stack(sff))
```

```python
import functools

import numpy as np
import jax
import jax.numpy as jnp
from jax import lax
from jax.experimental import pallas as pl
from jax.experimental.pallas import tpu as pltpu

F32 = jnp.float32
BF = jnp.bfloat16

D_MODEL = 1024
PAGE_SIZE = 128
ROPE_THETA = 500000.0
EPS = 1e-6

A_CH = D_MODEL // 2
CONV_A_WIDTH = 31

DH_B = 64
HB = (D_MODEL // 2) // DH_B
HKV_B = 2
ROT_B = DH_B // 4
H_IDX = 4
D_IDX = 64
ROT_IDX = D_IDX // 4
TOPK_MAX = 256
IDX_W_SCALE = (H_IDX * D_IDX) ** -0.5

VD_C = 64
HC = (D_MODEL // 2) // VD_C
NOPE_C = 64
ROPE_C = 32
DQK_C = NOPE_C + ROPE_C
Q_LORA = 3 * D_MODEL // 8
KV_LORA = D_MODEL // 4
MLA_SCALE = DQK_C ** -0.5

D_CH = D_MODEL // 2
D_GROUPS = 8
CHUNK = 128

D_FF = 11 * D_MODEL // 4
FFN_CONV_WIDTH = 3

LANES = 128
SUBLANES = 8
NEG = -1e30
VMEM_LIMIT = 56 * 1024 * 1024

E_A, E_G, E_Q, E_K, E_V, E_QI, E_KI, E_WI, E_END = 0, 512, 1024, 1536, 1664, 1792, 2048, 2176, 2304
O_QA, O_CKV, O_KPE, O_U, O_V, O_END = 0, 384, 640, 768, 1280, 1792


def _dot(a, b):
    return jnp.dot(a, b, preferred_element_type=F32)


def _dot_nt(a, b):
    return lax.dot_general(a, b, (((1,), (1,)), ((), ())), preferred_element_type=F32)


def _split2(x):
    hi = x.astype(BF)
    lo = (x - hi.astype(F32)).astype(BF)
    return hi, lo


def _dot2(x, m):
    hi, lo = _split2(x)
    return _dot(hi, m) + _dot(lo, m)


def _rms(x, g):
    return x * lax.rsqrt(jnp.mean(x * x, axis=-1, keepdims=True) + EPS) * g


def _rope128(x, c, sa, sb, half):
    return x * c + pltpu.roll(x, LANES - half, 1) * sa + pltpu.roll(x, half, 1) * sb


def _params(sem, vmem=VMEM_LIMIT):
    return pltpu.CompilerParams(dimension_semantics=sem, vmem_limit_bytes=vmem)


def _full(shape):
    n = len(shape)
    return pl.BlockSpec(shape, lambda *_: (0,) * n)


def _low_half(rows):
    return lax.broadcasted_iota(jnp.int32, (rows, LANES), 1) < (LANES // 2)


def _rope_tables(pos, n_rot, head_w):
    half = n_rot // 2
    inv_freq = jnp.power(jnp.float32(ROPE_THETA), -jnp.arange(half, dtype=F32) * (2.0 / n_rot))
    ang = pos.astype(F32)[:, None] * inv_freq[None, :]
    cos, sin = jnp.cos(ang), jnp.sin(ang)
    m = pos.shape[0]
    one = jnp.ones((m, head_w - n_rot), F32)
    zero = jnp.zeros((m, head_w - n_rot), F32)
    zh = jnp.zeros((m, half), F32)
    c = jnp.concatenate([cos, cos, one], axis=1)
    sa = jnp.concatenate([-sin, zh, zero], axis=1)
    sb = jnp.concatenate([zh, sin, zero], axis=1)
    rep = LANES // head_w
    return tuple(jnp.tile(t, (1, rep)) for t in (c, sa, sb))


def _indicator(width, group):
    lane = jnp.arange(width)[:, None] // group
    col = jnp.arange(LANES)[None, :]
    ind = (lane == col).astype(BF)
    return ind, ind.T


def _even_proj_body(x_ref, g_ref, w_ref, nrm_ref, ind_ref, indt_ref, c_ref, sa_ref, sb_ref,
                    a_ref, q_ref, k_ref, v_ref, ki_ref, wi_ref, qih_ref, qil_ref, kb_ref, vb_ref, kih_ref, kil_ref):
    h = _rms(x_ref[0], g_ref[...]).astype(BF)
    z = _dot(h, w_ref[...])
    a_ref[0] = z[:, E_A:E_G] * jax.nn.sigmoid(z[:, E_G:E_Q])
    c, sa, sb = c_ref[...], sa_ref[...], sb_ref[...]
    half = ROT_B // 2
    qk = z[:, E_Q:E_V]
    ssq = _dot2(qk * qk, ind_ref[...])
    r = lax.rsqrt(ssq / DH_B + EPS)
    qk = qk * _dot2(r, indt_ref[...]) * nrm_ref[...]
    for s in range(4):
        slab = _rope128(qk[:, s * LANES:(s + 1) * LANES], c, sa, sb, half)
        q_ref[0, :, s * LANES:(s + 1) * LANES] = (slab * (DH_B ** -0.5)).astype(BF)
    k = _rope128(qk[:, 4 * LANES:5 * LANES], c, sa, sb, half)
    k_ref[0] = k
    kb_ref[0] = k.astype(BF)
    v = z[:, E_V:E_QI]
    v_ref[0] = v
    vb_ref[0] = v.astype(BF)
    for s in range(2):
        lo = E_QI + s * LANES
        hi, lw = _split2(_rope128(z[:, lo:lo + LANES], c, sa, sb, half))
        qih_ref[0, :, s * LANES:(s + 1) * LANES] = hi
        qil_ref[0, :, s * LANES:(s + 1) * LANES] = lw
    ki = _rope128(z[:, E_KI:E_WI], c, sa, sb, half)
    ki_ref[0] = ki[:, :D_IDX]
    kih_ref[0], kil_ref[0] = _split2(ki)
    wi_ref[0] = z[:, E_WI:E_END] * IDX_W_SCALE


def _even_weights(w_in, q_norm, k_norm):
    cuts = [0, 2 * A_CH, 2 * A_CH + 512, 2 * A_CH + 640, 2 * A_CH + 768, 2 * A_CH + 1024, 2 * A_CH + 1088]
    glu = w_in[:, cuts[0]:cuts[1]]
    q = w_in[:, cuts[1]:cuts[2]]
    k = w_in[:, cuts[2]:cuts[3]]
    v = w_in[:, cuts[3]:cuts[4]]
    qi = w_in[:, cuts[4]:cuts[5]]
    ki = w_in[:, cuts[5]:cuts[6]]
    wi = w_in[:, cuts[6]:]
    q = q[:, _DSA_HEAD_PERM]
    pad = jnp.zeros((w_in.shape[0], LANES - H_IDX), w_in.dtype)
    w = jnp.concatenate([glu, q, k, v, qi, ki, ki, wi, pad], axis=1).astype(BF)
    nrm = jnp.concatenate([jnp.tile(q_norm, HB), jnp.tile(k_norm, HKV_B)])[None, :]
    return w, nrm


def _dsa_head_perm():
    j = np.arange(HB * DH_B)
    head = (j // LANES) + (HB // HKV_B) * ((j % LANES) // DH_B)
    return head * DH_B + (j % DH_B)


_DSA_HEAD_PERM = _dsa_head_perm()


def _even_proj(x, g, w, nrm, tables, tb):
    grp, t, _ = x.shape
    ind, indt = _indicator(5 * LANES, DH_B)
    blk = lambda width: pl.BlockSpec((1, tb, width), lambda i, j: (i, j, 0))
    tab = pl.BlockSpec((tb, LANES), lambda i, j: (j, 0))
    sds = lambda width, dt: jax.ShapeDtypeStruct((grp, t, width), dt)
    return pl.pallas_call(
        _even_proj_body,
        grid=(grp, t // tb),
        in_specs=[blk(D_MODEL), _full((1, D_MODEL)), _full(w.shape), _full(nrm.shape),
                  _full(ind.shape), _full(indt.shape), tab, tab, tab],
        out_specs=[blk(A_CH), blk(512), blk(LANES), blk(LANES), blk(D_IDX), blk(LANES),
                   blk(256), blk(256), blk(LANES), blk(LANES), blk(LANES), blk(LANES)],
        out_shape=[sds(A_CH, F32), sds(512, BF), sds(LANES, F32), sds(LANES, F32), sds(D_IDX, F32),
                   sds(LANES, F32), sds(256, BF), sds(256, BF), sds(LANES, BF), sds(LANES, BF),
                   sds(LANES, BF), sds(LANES, BF)],
        compiler_params=_params(("parallel", "parallel")),
        name="even_proj",
    )(x, g[None, :], w, nrm, ind, indt, *tables)


def _conv_tail_body(a_ref, prev_ref, w_ref, b_ref, g_ref, beta_ref, o_ref, buf, *, tb, pad, stride, rows, carry):
    j = pl.program_id(1)

    @pl.when(j == 0)
    def _():
        buf[0:pad] = prev_ref[0]

    buf[pad:pad + tb] = a_ref[0]
    base = pad - (CONV_A_WIDTH - 1) * stride
    for r0 in range(0, tb, rows):
        acc = jnp.broadcast_to(b_ref[...], (rows, A_CH))
        for tap in range(CONV_A_WIDTH):
            acc = acc + w_ref[tap:tap + 1, :] * buf[pl.ds(base + tap * stride + r0, rows), :]
        mu = jnp.mean(acc, axis=-1, keepdims=True)
        d = acc - mu
        var = jnp.mean(d * d, axis=-1, keepdims=True)
        y = d * lax.rsqrt(var + EPS) * g_ref[...] + beta_ref[...]
        o_ref[0, r0:r0 + rows, :] = (y * jax.nn.sigmoid(y)).astype(BF)
    if carry:
        buf[0:pad] = buf[tb:tb + pad]


def _conv_tail(a, prev, w, b, g, beta, tb, stride, rows):
    grp, t, _ = a.shape
    pad = prev.shape[1]
    wp = jnp.concatenate([w, jnp.zeros((1, A_CH), w.dtype)], axis=0)
    body = functools.partial(_conv_tail_body, tb=tb, pad=pad, stride=stride, rows=rows, carry=t > tb)
    return pl.pallas_call(
        body,
        grid=(grp, t // tb),
        in_specs=[pl.BlockSpec((1, tb, A_CH), lambda i, j: (i, j, 0)),
                  pl.BlockSpec((1, pad, A_CH), lambda i, j: (i, 0, 0)),
                  _full(wp.shape), _full((1, A_CH)), _full((1, A_CH)), _full((1, A_CH))],
        out_specs=pl.BlockSpec((1, tb, A_CH), lambda i, j: (i, j, 0)),
        out_shape=jax.ShapeDtypeStruct((grp, t, A_CH), BF),
        scratch_shapes=[pltpu.VMEM((pad + tb, A_CH), F32)],
        compiler_params=_params(("parallel", "arbitrary")),
        name="conv_tail",
    )(a, prev, wp, b[None, :], g[None, :], beta[None, :])


def _sort_keys(score):
    score = jnp.where(score == 0.0, 0.0, score)
    bits = pltpu.bitcast(score, jnp.int32)
    return jnp.where(bits < 0, bits ^ jnp.int32(0x7FFFFFFF), bits)


def _kth_largest(key_ref, width, nsel):
    rows = key_ref.shape[0]
    int_min = jnp.int32(-2 ** 31)

    def step(it, thr):
        cand = thr + lax.shift_left(jnp.int32(1), 31 - it)
        cnt = jnp.sum(jnp.where(key_ref[:, 0:width] >= cand, 1.0, 0.0), axis=-1, keepdims=True)
        return jnp.where(cnt >= nsel, cand, thr)

    return lax.fori_loop(0, 32, step, jnp.full((rows, 1), int_min, jnp.int32), unroll=2)


def _select_bias(key_ref, bias_ref, valid, tri_ref, width, nsel, chunk, row_ok=None):
    thr = _kth_largest(key_ref, width, nsel)
    key = key_ref[:, 0:width]
    cnt_gt = jnp.sum(jnp.where(key > thr, 1.0, 0.0), axis=-1, keepdims=True)
    cnt_eq = jnp.sum(jnp.where(key == thr, 1.0, 0.0), axis=-1, keepdims=True)
    need = nsel - cnt_gt
    bias_ref[:, 0:width] = jnp.where(jnp.logical_and(key >= thr, valid), 0.0, NEG)

    surplus = cnt_eq - need
    if row_ok is not None:
        surplus = jnp.where(row_ok, surplus, 0.0)

    @pl.when(jnp.max(surplus) > 0.0)
    def _():
        off = jnp.zeros_like(need)
        for c0 in range(0, width, chunk):
            kc = key_ref[:, c0:c0 + chunk]
            eq = kc == thr
            eqf = jnp.where(eq, 1.0, 0.0)
            rank = _dot(eqf.astype(BF), tri_ref[...]) + off
            take = jnp.logical_or(kc > thr, jnp.logical_and(eq, rank < need))
            bias_ref[:, c0:c0 + chunk] = jnp.where(jnp.logical_and(take, valid[:, c0:c0 + chunk]), 0.0, NEG)
            off = off + jnp.sum(eqf, axis=-1, keepdims=True)


def _strict_upper(n):
    return (jnp.arange(n)[:, None] < jnp.arange(n)[None, :]).astype(BF)


def _stack_heads(x2, low, keep_low):
    zero = jnp.zeros_like(x2)
    return jnp.where(low, x2, zero) if keep_low else jnp.where(low, zero, x2)


DSA_STACK_ROWS = 512


def _dsa_prompt_block(q_ref, qih_ref, qil_ref, wi_ref, k_ref, v_ref, kih_ref, kil_ref, tri_ref, o_ref,
                      key_ref, bias_ref, *, tq, sx, nsel):
    i = pl.program_id(1)
    low = _low_half(tq)
    qpos = i * tq + lax.broadcasted_iota(jnp.int32, (tq, sx), 0)
    kpos = lax.broadcasted_iota(jnp.int32, (tq, sx), 1)
    valid = kpos <= qpos
    per_dot = max(1, DSA_STACK_ROWS // tq)

    def stack(ref, heads, pick_low):
        return jnp.concatenate([_stack_heads(ref[0, :, (h // 2) * LANES:(h // 2 + 1) * LANES], low, pick_low(h))
                                for h in heads], axis=0)

    wi = wi_ref[0]
    score = jnp.zeros((tq, sx), F32)
    for h0 in range(0, H_IDX, per_dot):
        heads = range(h0, min(H_IDX, h0 + per_dot))
        hi = stack(qih_ref, heads, lambda h: h % 2 == 0)
        lw = stack(qil_ref, heads, lambda h: h % 2 == 0)
        logits = (_dot_nt(hi, kih_ref[0, 0:sx, :]) + _dot_nt(hi, kil_ref[0, 0:sx, :])
                  + _dot_nt(lw, kih_ref[0, 0:sx, :]))
        for n, h in enumerate(heads):
            score = score + wi[:, h:h + 1] * jnp.maximum(logits[n * tq:(n + 1) * tq], 0.0)
    score = jnp.where(valid, score, -jnp.inf)
    key_ref[:, 0:sx] = _sort_keys(score)
    _select_bias(key_ref, bias_ref, valid, tri_ref, sx, nsel, LANES)

    bias = bias_ref[:, 0:sx]
    n_pairs = HB // HKV_B
    for p0 in range(0, n_pairs, per_dot):
        pairs = range(p0, min(n_pairs, p0 + per_dot))
        biasn = jnp.concatenate([bias] * len(pairs), axis=0)
        outs = []
        for g in range(HKV_B):
            qs = stack(q_ref, [2 * p for p in pairs], lambda h: g == 0)
            s = _dot_nt(qs, k_ref[0, 0:sx, :]) + biasn
            e = jnp.exp(s - jnp.max(s, axis=-1, keepdims=True))
            l = jnp.sum(e, axis=-1, keepdims=True)
            outs.append(_dot(e.astype(BF), v_ref[0, 0:sx, :]) / l)
        for n, p in enumerate(pairs):
            o_ref[0, :, p * LANES:(p + 1) * LANES] = jnp.where(
                low, outs[0][n * tq:(n + 1) * tq], outs[1][n * tq:(n + 1) * tq]).astype(BF)


def _causal_extents(n_blocks, tq, s):
    nb = 4 if n_blocks % 4 == 0 else 1
    per = n_blocks // nb
    return per, [min(s, (c + 1) * per * tq) for c in range(nb)]


def _dsa_prompt_body(*refs, tq, s, nsel):
    per, extents = _causal_extents(s // tq, tq, s)
    i = pl.program_id(1)
    for c, sx in enumerate(extents):
        pl.when(i // per == c)(functools.partial(_dsa_prompt_block, *refs, tq=tq, sx=sx, nsel=nsel))


def _dsa_prompt(q, qih, qil, wi, kb, vb, kih, kil, tq):
    n, s, _ = q.shape
    nsel = min(TOPK_MAX, s // 4)
    tri = _strict_upper(LANES)
    qblk = lambda width: pl.BlockSpec((1, tq, width), lambda i, j: (i, j, 0))
    sblk = lambda width: pl.BlockSpec((1, s, width), lambda i, j: (i, 0, 0))
    return pl.pallas_call(
        functools.partial(_dsa_prompt_body, tq=tq, s=s, nsel=nsel),
        grid=(n, s // tq),
        in_specs=[qblk(512), qblk(256), qblk(256), qblk(LANES), sblk(LANES), sblk(LANES), sblk(LANES),
                  sblk(LANES), _full(tri.shape)],
        out_specs=qblk(512),
        out_shape=jax.ShapeDtypeStruct((n, s, 512), BF),
        scratch_shapes=[pltpu.VMEM((tq, s), jnp.int32), pltpu.VMEM((tq, s), F32)],
        compiler_params=_params(("parallel", "parallel")),
        name="dsa_prompt",
    )(q, qih, qil, wi, kb, vb, kih, kil, tri)


def _out_proj_body(x_ref, a_ref, b_ref, wa_ref, wb_ref, o_ref):
    o_ref[0] = x_ref[0] + _dot(a_ref[0], wa_ref[...]) + _dot(b_ref[0], wb_ref[...])


def _out_proj(x, a, b, wa, wb, tb):
    grp, t, _ = x.shape
    blk = lambda width: pl.BlockSpec((1, tb, width), lambda i, j: (i, j, 0))
    return pl.pallas_call(
        _out_proj_body,
        grid=(grp, t // tb),
        in_specs=[blk(D_MODEL), blk(a.shape[-1]), blk(b.shape[-1]), _full(wa.shape), _full(wb.shape)],
        out_specs=blk(D_MODEL),
        out_shape=jax.ShapeDtypeStruct(x.shape, F32),
        compiler_params=_params(("parallel", "parallel")),
        name="out_proj",
    )(x, a, b, wa, wb)


FFN_TILE = 256


def _ffn_body(x_ref, g_ref, wu_ref, cw_ref, cb_ref, wd_ref, prev_ref, o_ref, tail_ref,
              hbuf, carry, ubuf, acc, *, tb, pad, stride):
    j = pl.program_id(1)
    hbuf[...] = _rms(x_ref[0], g_ref[...]).astype(BF)

    @pl.when(j == 0)
    def _():
        carry[...] = prev_ref[0]

    acc[...] = x_ref[0]
    for f in range(D_FF // FFN_TILE):
        halves = []
        for part in range(2):
            c0 = part * D_FF + f * FFN_TILE
            u = _dot(hbuf[...], wu_ref[:, c0:c0 + FFN_TILE])
            ubuf[0:pad] = carry[:, c0:c0 + FFN_TILE]
            ubuf[pad:pad + tb] = u
            y = (cw_ref[0:1, c0:c0 + FFN_TILE] * ubuf[pl.ds(pad - 2 * stride, tb), :]
                 + cw_ref[1:2, c0:c0 + FFN_TILE] * ubuf[pl.ds(pad - stride, tb), :]
                 + cw_ref[2:3, c0:c0 + FFN_TILE] * u + cb_ref[:, c0:c0 + FFN_TILE])
            carry[:, c0:c0 + FFN_TILE] = ubuf[tb:tb + pad]
            halves.append(y)
        act = (halves[1] * jax.nn.sigmoid(halves[1]) * halves[0]).astype(BF)
        acc[...] += _dot(act, wd_ref[f * FFN_TILE:(f + 1) * FFN_TILE, :])
    o_ref[0] = acc[...]

    @pl.when(j == pl.num_programs(1) - 1)
    def _():
        tail_ref[0] = carry[...]


def _ffn(x, g, wu, cw, cb, wd, prev, tb, stride):
    grp, t, _ = x.shape
    pad = prev.shape[1]
    cwp = jnp.concatenate([cw, jnp.zeros((SUBLANES - FFN_CONV_WIDTH, 2 * D_FF), cw.dtype)], axis=0)
    blk = pl.BlockSpec((1, tb, D_MODEL), lambda i, j: (i, j, 0))
    pblk = pl.BlockSpec((1, pad, 2 * D_FF), lambda i, j: (i, 0, 0))
    once = lambda shape: pl.BlockSpec(shape, lambda i, j: (0,) * len(shape), pipeline_mode=pl.Buffered(1))
    return pl.pallas_call(
        functools.partial(_ffn_body, tb=tb, pad=pad, stride=stride),
        grid=(grp, t // tb),
        in_specs=[blk, _full((1, D_MODEL)), once(wu.shape), _full(cwp.shape), _full((1, 2 * D_FF)),
                  once(wd.shape), pblk],
        out_specs=[blk, pblk],
        out_shape=[jax.ShapeDtypeStruct(x.shape, F32), jax.ShapeDtypeStruct(prev.shape, F32)],
        scratch_shapes=[pltpu.VMEM((tb, D_MODEL), BF), pltpu.VMEM((pad, 2 * D_FF), F32),
                        pltpu.VMEM((pad + tb, FFN_TILE), F32), pltpu.VMEM((tb, D_MODEL), F32)],
        compiler_params=_params(("parallel", "arbitrary")),
        name="conv_ffn",
    )(x, g[None, :], wu, cwp, cb[None, :], wd, prev)


def _head_norm128(x, gain, ind_ref, indt_ref, dim):
    ssq = _dot2(x * x, ind_ref[...])
    r = lax.rsqrt(ssq / dim + EPS)
    return x * _dot2(r, indt_ref[...]) * gain


def _odd_proj_body(x_ref, g_ref, w_ref, qan_ref, wqb_ref, kvn_ref, qn_ref, kn_ref, ind_ref, indt_ref,
                   c_ref, sa_ref, sb_ref, glg_ref, glb_ref, wuk_ref, wuv_ref,
                   ckv_ref, kpe_ref, u_ref, v_ref, q_ref, k_ref, vv_ref):
    h = _rms(x_ref[0], g_ref[...]).astype(BF)
    z = _dot(h, w_ref[...])
    c, sa, sb = c_ref[...], sa_ref[...], sb_ref[...]
    half = ROPE_C // 2

    qa = _rms(z[:, O_QA:O_CKV], qan_ref[...]).astype(BF)
    q = _head_norm128(_dot(qa, wqb_ref[...]), qn_ref[...], ind_ref, indt_ref, DQK_C)
    for hd in range(HC):
        sl = slice(hd * LANES, (hd + 1) * LANES)
        q_ref[0, :, sl] = (_rope128(q[:, sl], c, sa, sb, half) * MLA_SCALE).astype(BF)

    ckv = _rms(z[:, O_CKV:O_KPE], kvn_ref[...])
    ckv_ref[0] = ckv
    cb = ckv.astype(BF)
    kpe = z[:, O_KPE:O_U]
    kpe_ref[0] = kpe[:, :ROPE_C]
    kfull = _dot(cb, wuk_ref[...]) + jnp.concatenate([kpe] * HC, axis=1)
    k = _head_norm128(kfull, kn_ref[...], ind_ref, indt_ref, DQK_C)
    for hd in range(HC):
        sl = slice(hd * LANES, (hd + 1) * LANES)
        k_ref[0, :, sl] = _rope128(k[:, sl], c, sa, sb, half).astype(BF)
    vv_ref[0] = _dot(cb, wuv_ref[...]).astype(BF)

    zz = jax.nn.gelu(z[:, O_U:O_END])
    u_ref[0] = zz[:, :D_CH]
    vz = zz[:, D_CH:]
    mu = jnp.mean(vz, axis=-1, keepdims=True)
    d = vz - mu
    var = jnp.mean(d * d, axis=-1, keepdims=True)
    v_ref[0] = d * lax.rsqrt(var + EPS) * glg_ref[...] + glb_ref[...]


def _pad_heads(w, lead):
    z32 = jnp.zeros(lead + (HC, ROPE_C), w.dtype)
    return jnp.concatenate([z32, w, z32], axis=-1).reshape(lead + (HC * LANES,))


def _odd_weights(w_in, w_qb, w_uk, w_uv, q_norm, k_norm):
    d = w_in.shape[0]
    cuts = [Q_LORA, Q_LORA + KV_LORA, Q_LORA + KV_LORA + ROPE_C]
    pad = jnp.zeros((d, LANES - ROPE_C), w_in.dtype)
    w = jnp.concatenate([w_in[:, :cuts[2]], pad, w_in[:, cuts[2]:]], axis=1).astype(BF)
    qb = w_qb.reshape(Q_LORA, HC, DQK_C)
    qb = jnp.concatenate([qb, jnp.zeros((Q_LORA, HC, LANES - DQK_C), w_qb.dtype)], axis=-1)
    wqb = qb.reshape(Q_LORA, HC * LANES).astype(BF)
    wuk_pad = _pad_heads(w_uk, (KV_LORA,)).astype(BF)
    wuk = w_uk.reshape(KV_LORA, HC * NOPE_C).astype(BF)
    wuv = w_uv.reshape(KV_LORA, HC * VD_C).astype(BF)
    gain = lambda g: jnp.tile(jnp.concatenate([g, jnp.zeros((LANES - DQK_C,), g.dtype)]), HC)[None, :]
    return w, wqb, wuk_pad, wuk, wuv, gain(q_norm), gain(k_norm)


def _odd_proj(x, g, w, wqb, wuk_pad, wuv, qan, kvn, qn, kn, glg, glb, tables, tb):
    grp, t, _ = x.shape
    ind, indt = _indicator(HC * LANES, LANES)
    blk = lambda width: pl.BlockSpec((1, tb, width), lambda i, j: (i, j, 0))
    tab = pl.BlockSpec((tb, LANES), lambda i, j: (j, 0))
    sds = lambda width, dt: jax.ShapeDtypeStruct((grp, t, width), dt)
    row = lambda v: v[None, :]
    return pl.pallas_call(
        _odd_proj_body,
        grid=(grp, t // tb),
        in_specs=[blk(D_MODEL), _full((1, D_MODEL)), _full(w.shape), _full((1, Q_LORA)), _full(wqb.shape),
                  _full((1, KV_LORA)), _full(qn.shape), _full(kn.shape), _full(ind.shape), _full(indt.shape),
                  tab, tab, tab, _full((1, D_CH)), _full((1, D_CH)), _full(wuk_pad.shape), _full(wuv.shape)],
        out_specs=[blk(KV_LORA), blk(ROPE_C), blk(D_CH), blk(D_CH), blk(HC * LANES), blk(HC * LANES),
                   blk(HC * VD_C)],
        out_shape=[sds(KV_LORA, F32), sds(ROPE_C, F32), sds(D_CH, F32), sds(D_CH, F32),
                   sds(HC * LANES, BF), sds(HC * LANES, BF), sds(HC * VD_C, BF)],
        compiler_params=_params(("parallel", "parallel")),
        name="odd_proj",
    )(x, row(g), w, row(qan), wqb, row(kvn), qn, kn, ind, indt, *tables, row(glg), row(glb), wuk_pad, wuv)


def _gmlp_prompt_body(u_ref, v_ref, w_ref, b_ref, o_ref, *, tb):
    rows = D_GROUPS * CHUNK
    t_in = lax.broadcasted_iota(jnp.int32, (rows, CHUNK), 0) % CHUNK
    s_in = lax.broadcasted_iota(jnp.int32, (rows, CHUNK), 1)
    w = jnp.where(s_in <= t_in, w_ref[...], 0.0).astype(BF)
    grp = lax.broadcasted_iota(jnp.int32, (CHUNK, D_CH), 1) // (D_CH // D_GROUPS)
    for c0 in range(0, tb, CHUNK):
        y = _dot(w, v_ref[0, c0:c0 + CHUNK, :].astype(BF))
        mix = b_ref[...]
        for gi in range(D_GROUPS):
            mix = mix + jnp.where(grp == gi, y[gi * CHUNK:(gi + 1) * CHUNK], 0.0)
        o_ref[0, c0:c0 + CHUNK, :] = (u_ref[0, c0:c0 + CHUNK, :] * mix).astype(BF)


def _gmlp_prompt(u, v, w_s, b_s, tb):
    n, s, _ = u.shape
    w = w_s.reshape(D_GROUPS * CHUNK, CHUNK)
    b = jnp.repeat(b_s.T, D_CH // D_GROUPS, axis=1)
    blk = pl.BlockSpec((1, tb, D_CH), lambda i, j: (i, j, 0))
    return pl.pallas_call(
        functools.partial(_gmlp_prompt_body, tb=tb),
        grid=(n, s // tb),
        in_specs=[blk, blk, _full(w.shape), _full(b.shape)],
        out_specs=blk,
        out_shape=jax.ShapeDtypeStruct(u.shape, BF),
        compiler_params=_params(("parallel", "parallel")),
        name="gmlp_prompt",
    )(u, v, w, b)


def _gmlp_sample_body(u_ref, v_ref, w_ref, b_ref, o_ref, *, nb, nt):
    for t in range(nt):
        mix = jnp.broadcast_to(b_ref[t:t + 1, :], (nb, D_CH))
        for s in range(t + 1):
            mix = mix + w_ref[t * nt + s:t * nt + s + 1, :] * v_ref[s * nb:(s + 1) * nb, :]
        o_ref[t * nb:(t + 1) * nb, :] = (u_ref[t * nb:(t + 1) * nb, :] * mix).astype(BF)


def _gmlp_sample(u, v, w_s, b_s, nb, nt):
    lane = D_CH // D_GROUPS
    w = jnp.repeat(jnp.transpose(w_s[:, :nt, :nt], (1, 2, 0)).reshape(nt * nt, D_GROUPS), lane, axis=1)
    b = jnp.repeat(b_s.T[:nt], lane, axis=1)
    return pl.pallas_call(
        functools.partial(_gmlp_sample_body, nb=nb, nt=nt),
        out_shape=jax.ShapeDtypeStruct(u.shape, BF),
        name="gmlp_sample",
    )(u, v, w, b)


def _mla_prompt_block(q_ref, k_ref, v_ref, o_ref, *, tq, sx):
    i = pl.program_id(1)
    low = _low_half(tq)
    qpos = i * tq + lax.broadcasted_iota(jnp.int32, (tq, sx), 0)
    kpos = lax.broadcasted_iota(jnp.int32, (tq, sx), 1)
    bias = jnp.where(kpos <= qpos, 0.0, NEG)
    for p in range(HC // 2):
        outs = []
        for hd in (2 * p, 2 * p + 1):
            sl = slice(hd * LANES, (hd + 1) * LANES)
            s = _dot_nt(q_ref[0, :, sl], k_ref[0, 0:sx, sl]) + bias
            e = jnp.exp(s - jnp.max(s, axis=-1, keepdims=True))
            l = jnp.sum(e, axis=-1, keepdims=True)
            outs.append(_dot(e.astype(BF), v_ref[0, 0:sx, p * LANES:(p + 1) * LANES]) / l)
        o_ref[0, :, p * LANES:(p + 1) * LANES] = jnp.where(low, outs[0], outs[1]).astype(BF)


def _mla_prompt_body(*refs, tq, s):
    per, extents = _causal_extents(s // tq, tq, s)
    i = pl.program_id(1)
    for c, sx in enumerate(extents):
        pl.when(i // per == c)(functools.partial(_mla_prompt_block, *refs, tq=tq, sx=sx))


def _mla_prompt(q, k, v, tq):
    n, s, _ = q.shape
    return pl.pallas_call(
        functools.partial(_mla_prompt_body, tq=tq, s=s),
        grid=(n, s // tq),
        in_specs=[pl.BlockSpec((1, tq, HC * LANES), lambda i, j: (i, j, 0)),
                  pl.BlockSpec((1, s, HC * LANES), lambda i, j: (i, 0, 0)),
                  pl.BlockSpec((1, s, HC * VD_C), lambda i, j: (i, 0, 0))],
        out_specs=pl.BlockSpec((1, tq, HC * VD_C), lambda i, j: (i, j, 0)),
        out_shape=jax.ShapeDtypeStruct((n, s, HC * VD_C), BF),
        compiler_params=_params(("parallel", "parallel")),
        name="mla_prompt",
    )(q, k, v)


PAGES_PER_STEP = 16


def _dot_tn(a, b):
    return lax.dot_general(a, b, (((0,), (0,)), ((), ())), preferred_element_type=F32)


def _page_specs(cache_shape, layer):
    _, _, rows, width = cache_shape

    def spec(slot):
        return pl.BlockSpec((None, None, rows, width),
                            lambda s, j, pt: (layer, pt[s, j * PAGES_PER_STEP + slot], 0, 0))

    return [spec(slot) for slot in range(PAGES_PER_STEP)]


def _cat_pages(refs, axis):
    return jnp.concatenate([r[...] for r in refs], axis=axis)


def _idx_scores(qh, ql, wi, kh, kl):
    both = _dot(jnp.concatenate([qh, ql], axis=0), kh)
    rows = qh.shape[0]
    logits = both[0:rows] + both[rows:2 * rows] + _dot(qh, kl)
    score = jnp.zeros((SUBLANES, logits.shape[1]), F32)
    for h in range(H_IDX):
        sl = slice(h * SUBLANES, (h + 1) * SUBLANES)
        score = score + wi[sl, 0:1] * jnp.maximum(logits[sl], 0.0)
    return score


def _dsa_sample_score_body(pt_ref, qh_ref, ql_ref, wi_ref, *rest):
    del pt_ref
    pages, o_ref = rest[:PAGES_PER_STEP], rest[PAGES_PER_STEP]
    kh, kl = _split2(_cat_pages(pages, 1))
    o_ref[0] = _idx_scores(qh_ref[0], ql_ref[0], wi_ref[0], kh, kl)


def _dsa_sample_scores(page_table, qh, ql, wi, cache_ki, layer):
    n, n_pages = page_table.shape
    steps = n_pages // PAGES_PER_STEP
    span = PAGES_PER_STEP * PAGE_SIZE
    per_seq = lambda shape: pl.BlockSpec((1,) + shape, lambda s, j, pt: (s, 0, 0))
    grid_spec = pltpu.PrefetchScalarGridSpec(
        num_scalar_prefetch=1,
        grid=(n, steps),
        in_specs=[per_seq((4 * SUBLANES, D_IDX)), per_seq((4 * SUBLANES, D_IDX)), per_seq((4 * SUBLANES, LANES))]
        + _page_specs(cache_ki.shape, layer),
        out_specs=pl.BlockSpec((1, SUBLANES, span), lambda s, j, pt: (s, 0, j)),
    )
    return pl.pallas_call(
        _dsa_sample_score_body,
        grid_spec=grid_spec,
        out_shape=jax.ShapeDtypeStruct((n, SUBLANES, n_pages * PAGE_SIZE), F32),
        compiler_params=_params(("parallel", "arbitrary")),
        name="dsa_sample_scores",
    )(page_table, qh, ql, wi, *([cache_ki] * PAGES_PER_STEP))


def _dsa_sample_attend_body(pt_ref, score_ref, qh_ref, ql_ref, wi_ref, knh_ref, knl_ref, q_ref, kn_ref, vn_ref,
                            tri_ref, *rest, past, nsel, nt):
    del pt_ref
    kpages = rest[:PAGES_PER_STEP]
    vpages = rest[PAGES_PER_STEP:2 * PAGES_PER_STEP]
    o_ref, key_ref, bias_ref, q64_ref, m_ref, l_ref, acc_ref = rest[2 * PAGES_PER_STEP:]
    j = pl.program_id(1)
    span = PAGES_PER_STEP * PAGE_SIZE
    width = past + PAGE_SIZE
    rows8 = SUBLANES

    @pl.when(j == 0)
    def _():
        new = _idx_scores(qh_ref[0], ql_ref[0], wi_ref[0], knh_ref[0], knl_ref[0])
        row = lax.broadcasted_iota(jnp.int32, (rows8, width), 0)
        kpos = lax.broadcasted_iota(jnp.int32, (rows8, width), 1)
        valid = kpos <= past + row
        key_ref[:, 0:past] = _sort_keys(score_ref[0])
        key_ref[:, past:width] = _sort_keys(jnp.where(valid[:, past:width], new, -jnp.inf))
        row_ok = lax.broadcasted_iota(jnp.int32, (rows8, 1), 0) < nt
        _select_bias(key_ref, bias_ref, valid, tri_ref, width, nsel, LANES, row_ok)
        low = _low_half(rows8)
        per = HB // HKV_B
        q8 = q_ref[0].astype(F32)
        q64_ref[...] = jnp.concatenate(
            [_stack_heads(q8[:, (h % per) * LANES:(h % per + 1) * LANES], low, h < per) for h in range(HB)],
            axis=0).astype(BF)
        m_ref[...] = jnp.full(m_ref.shape, NEG, F32)
        l_ref[...] = jnp.zeros(l_ref.shape, F32)
        acc_ref[...] = jnp.zeros(acc_ref.shape, F32)

    def fold(bias8, k_t, v_t):
        s = _dot(q64_ref[...], k_t) + jnp.concatenate([bias8] * HB, axis=0)
        m_old = m_ref[...]
        m_new = jnp.maximum(m_old, jnp.max(s, axis=-1, keepdims=True))
        alpha = jnp.exp(m_old - m_new)
        e = jnp.exp(s - m_new)
        l_ref[...] = alpha * l_ref[...] + jnp.sum(e, axis=-1, keepdims=True)
        acc_ref[...] = alpha * acc_ref[...] + _dot_nt(e.astype(BF), v_t)
        m_ref[...] = m_new

    off = pl.multiple_of(j * span, span)
    fold(bias_ref[:, pl.ds(off, span)], _cat_pages(kpages, 1).astype(BF), _cat_pages(vpages, 1).astype(BF))

    @pl.when(j == pl.num_programs(1) - 1)
    def _():
        fold(bias_ref[:, past:width], kn_ref[0], vn_ref[0])
        out = acc_ref[...] / l_ref[...]
        low = _low_half(rows8)
        per = HB // HKV_B
        for p in range(per):
            o_ref[0, :, p * LANES:(p + 1) * LANES] = jnp.where(
                low, out[p * rows8:(p + 1) * rows8], out[(per + p) * rows8:(per + p + 1) * rows8]).astype(BF)


def _dsa_sample_attend(page_table, score, qh, ql, wi, knh, knl, q8, kn, vn, cache_k, cache_v, layer, nt):
    n, n_pages = page_table.shape
    past = n_pages * PAGE_SIZE
    nsel = min(TOPK_MAX, (past + nt) // 4)
    tri = _strict_upper(LANES)
    width = past + PAGE_SIZE
    per_seq = lambda shape: pl.BlockSpec((1,) + shape, lambda s, j, pt: (s, 0, 0))
    grid_spec = pltpu.PrefetchScalarGridSpec(
        num_scalar_prefetch=1,
        grid=(n, n_pages // PAGES_PER_STEP),
        in_specs=[per_seq((SUBLANES, past)), per_seq((4 * SUBLANES, D_IDX)), per_seq((4 * SUBLANES, D_IDX)),
                  per_seq((4 * SUBLANES, LANES)), per_seq((D_IDX, PAGE_SIZE)), per_seq((D_IDX, PAGE_SIZE)),
                  per_seq((SUBLANES, HB * DH_B)), per_seq((LANES, PAGE_SIZE)), per_seq((LANES, PAGE_SIZE)),
                  pl.BlockSpec(tri.shape, lambda s, j, pt: (0, 0))]
        + _page_specs(cache_k.shape, layer) + _page_specs(cache_v.shape, layer),
        out_specs=per_seq((SUBLANES, HB * DH_B)),
        scratch_shapes=[pltpu.VMEM((SUBLANES, width), jnp.int32), pltpu.VMEM((SUBLANES, width), F32),
                        pltpu.VMEM((HB * SUBLANES, LANES), BF), pltpu.VMEM((HB * SUBLANES, 1), F32),
                        pltpu.VMEM((HB * SUBLANES, 1), F32), pltpu.VMEM((HB * SUBLANES, LANES), F32)],
    )
    return pl.pallas_call(
        functools.partial(_dsa_sample_attend_body, past=past, nsel=nsel, nt=nt),
        grid_spec=grid_spec,
        out_shape=jax.ShapeDtypeStruct((n, SUBLANES, HB * DH_B), BF),
        compiler_params=_params(("parallel", "arbitrary")),
        name="dsa_sample_attend",
    )(page_table, score, qh, ql, wi, knh, knl, q8, kn, vn, tri,
      *([cache_k] * PAGES_PER_STEP), *([cache_v] * PAGES_PER_STEP))


MLA_COLS = HC * SUBLANES


def _col_to_rows(row_vec):
    return jnp.transpose(jnp.broadcast_to(row_vec, (LANES, LANES)))


def _mla_sample_body(pt_ref, q_ref, kn_ref, vn_ref, wukp_ref, wuk_ref, wuv_ref, gk_ref, ind_ref, ct_ref, st_ref,
                     *rest, nt):
    del pt_ref
    cpages = rest[:PAGES_PER_STEP]
    ppages = rest[PAGES_PER_STEP:2 * PAGES_PER_STEP]
    o_ref, qbd_ref, qabs_ref, qrot_ref, m_ref, l_ref, acc_ref = rest[2 * PAGES_PER_STEP:]
    j = pl.program_id(1)
    cols = MLA_COLS
    dup = lambda x: jnp.concatenate([x, x], axis=1)

    @pl.when(j == 0)
    def _():
        q8 = q_ref[0].astype(F32)
        rhead = lax.broadcasted_iota(jnp.int32, (cols, HC * LANES), 0) // SUBLANES
        lhead = lax.broadcasted_iota(jnp.int32, (cols, HC * LANES), 1) // LANES
        qbd = jnp.where(rhead == lhead, jnp.concatenate([q8] * HC, axis=0), 0.0)
        qbd_ref[...] = qbd.astype(BF)
        qabs_ref[...] = dup(_dot_nt(wukp_ref[...], (qbd * gk_ref[...]).astype(BF))).astype(BF)
        folded = qbd[:, 0:LANES]
        for hd in range(1, HC):
            folded = folded + qbd[:, hd * LANES:(hd + 1) * LANES]
        lane = lax.broadcasted_iota(jnp.int32, (cols, LANES), 1)
        half = ROPE_C // 2
        swapped = jnp.where(lane < half, pltpu.roll(folded, LANES - half, 1),
                            jnp.where(lane < ROPE_C, -pltpu.roll(folded, half, 1), 0.0))
        gpe = gk_ref[:, 0:LANES]
        eye = (lax.broadcasted_iota(jnp.int32, (ROPE_C, LANES), 0)
               == lax.broadcasted_iota(jnp.int32, (ROPE_C, LANES), 1)).astype(BF)
        qpe_t = _dot_nt(eye, (folded * gpe).astype(BF))
        qsw_t = _dot_nt(eye, (swapped * gpe).astype(BF))
        zero = jnp.zeros((ROPE_C, 2 * cols), F32)
        one = jnp.ones((ROPE_C, 2 * cols), F32)
        qrot_ref[...] = jnp.concatenate([
            jnp.concatenate([dup(qpe_t), zero], axis=1),
            jnp.concatenate([dup(qsw_t), zero], axis=1),
            jnp.concatenate([zero, one], axis=1)], axis=0).astype(BF)
        m_ref[...] = jnp.full(m_ref.shape, NEG, F32)
        l_ref[...] = jnp.zeros(l_ref.shape, F32)
        acc_ref[...] = jnp.zeros(acc_ref.shape, F32)

    cb = _cat_pages(cpages, 0).astype(BF)
    kpe = _cat_pages(ppages, 1)
    kn = _dot(cb, wuk_ref[...])
    ssq = _dot((kn * kn).astype(BF), ind_ref[...])
    feats = jnp.concatenate([kpe * ct_ref[...], kpe * st_ref[...], kpe * kpe], axis=0).astype(BF)
    rot = _dot_tn(feats, qrot_ref[...])
    raw = _dot(cb, qabs_ref[...]) + rot[:, 0:LANES]
    s = lax.rsqrt((ssq + rot[:, LANES:2 * LANES]) / DQK_C + EPS) * raw

    m_old = m_ref[...]
    m_new = jnp.maximum(m_old, jnp.max(s, axis=0, keepdims=True))
    alpha = jnp.exp(m_old - m_new)
    e = jnp.exp(s - m_new)
    l_ref[...] = alpha * l_ref[...] + jnp.sum(e, axis=0, keepdims=True)
    acc_ref[...] = dup(_col_to_rows(alpha)) * acc_ref[...] + _dot_tn(e.astype(BF), cb)
    m_ref[...] = m_new

    @pl.when(j == pl.num_programs(1) - 1)
    def _():
        s_new = dup(_dot_nt(kn_ref[0], qbd_ref[...]))
        tok = lax.broadcasted_iota(jnp.int32, s_new.shape, 0)
        slot = lax.broadcasted_iota(jnp.int32, s_new.shape, 1) % SUBLANES
        s_new = jnp.where(jnp.logical_and(tok <= slot, tok < nt), s_new, NEG)
        m_old = m_ref[...]
        m_fin = jnp.maximum(m_old, jnp.max(s_new, axis=0, keepdims=True))
        alpha = jnp.exp(m_old - m_fin)
        e = jnp.exp(s_new - m_fin)
        l = alpha * l_ref[...] + jnp.sum(e, axis=0, keepdims=True)
        lat = (_col_to_rows(alpha / l)[:, 0:1] * acc_ref[...]).astype(BF)
        new = _dot_tn((e / l).astype(BF), vn_ref[0])
        out = _dot(lat, wuv_ref[...]) + new
        lhead = lax.broadcasted_iota(jnp.int32, (SUBLANES, HC * VD_C), 1) // VD_C
        res = jnp.zeros((SUBLANES, HC * VD_C), F32)
        for hd in range(HC):
            res = res + jnp.where(lhead == hd, out[hd * SUBLANES:(hd + 1) * SUBLANES], 0.0)
        o_ref[0] = res.astype(BF)


def _mla_sample(page_table, q8, kn, vn, wuk_pad, wuk, wuv, gk, ctab, stab, cache_ckv, cache_kpe, layer, nt):
    n, n_pages = page_table.shape
    span = PAGES_PER_STEP * PAGE_SIZE
    cols = MLA_COLS
    head_of_col = (jnp.arange(2 * cols) % cols) // SUBLANES
    ind = (jnp.arange(HC * NOPE_C)[:, None] // NOPE_C == head_of_col[None, :]).astype(BF)
    per_seq = lambda shape: pl.BlockSpec((1,) + shape, lambda s, j, pt: (s, 0, 0))
    const = lambda a: pl.BlockSpec(a.shape, lambda s, j, pt: (0,) * a.ndim)
    tab = pl.BlockSpec((ROPE_C, span), lambda s, j, pt: (0, j))
    grid_spec = pltpu.PrefetchScalarGridSpec(
        num_scalar_prefetch=1,
        grid=(n, n_pages // PAGES_PER_STEP),
        in_specs=[per_seq((SUBLANES, HC * LANES)), per_seq((PAGE_SIZE, HC * LANES)), per_seq((PAGE_SIZE, HC * VD_C)),
                  const(wuk_pad), const(wuk), const(wuv), const(gk), const(ind), tab, tab]
        + _page_specs(cache_ckv.shape, layer) + _page_specs(cache_kpe.shape, layer),
        out_specs=per_seq((SUBLANES, HC * VD_C)),
        scratch_shapes=[pltpu.VMEM((cols, HC * LANES), BF), pltpu.VMEM((KV_LORA, 2 * cols), BF),
                        pltpu.VMEM((3 * ROPE_C, 4 * cols), BF),
                        pltpu.VMEM((1, 2 * cols), F32), pltpu.VMEM((1, 2 * cols), F32),
                        pltpu.VMEM((2 * cols, KV_LORA), F32)],
    )
    return pl.pallas_call(
        functools.partial(_mla_sample_body, nt=nt),
        grid_spec=grid_spec,
        out_shape=jax.ShapeDtypeStruct((n, SUBLANES, HC * VD_C), BF),
        compiler_params=_params(("parallel", "arbitrary")),
        name="mla_sample",
    )(page_table, q8, kn, vn, wuk_pad, wuk, wuv, gk, ind, ctab, stab,
      *([cache_ckv] * PAGES_PER_STEP), *([cache_kpe] * PAGES_PER_STEP))


PROMPT_ROWS = 512
CONV_ROWS = 256
DSA_QBLOCK = 256
MLA_QBLOCK = 256


def kernel(x_prompt, x_sample, cache_dsa_k, cache_dsa_v, cache_dsa_kidx, state_conv_a, cache_mla_ckv, cache_mla_kpe, state_ffn_conv, page_table, norm_mix, norm_ffn, w_in_e, conv_a_w, conv_a_b, conv_a_ln_g, conv_a_ln_b, q_norm_b, k_norm_b, w_out_e, w_in_o, q_a_norm, w_qb, kv_a_norm, w_uk, w_uv, q_norm_c, k_norm_c, gmlp_ln_g, gmlp_ln_b, w_spatial, b_spatial, w_out_o, w_up, ffn_conv_w, ffn_conv_b, w_down):
    bsz, seq, _ = x_prompt.shape
    nb, nt, _ = x_sample.shape
    depth = norm_mix.shape[0]
    past = page_table.shape[1] * PAGE_SIZE
    srows = nt * nb
    pos_p = jnp.arange(seq)
    pos_s = past + jnp.repeat(jnp.arange(nt), nb)

    def to_rows(a):
        return jnp.transpose(a, (1, 0, 2)).reshape(1, a.shape[1] * nb, a.shape[2])

    def to_seq(a):
        return jnp.transpose(a.reshape(-1, nb, a.shape[-1]), (1, 0, 2))

    def pad_rows(a, rows):
        return jnp.pad(a, ((0, 0), (0, rows - a.shape[1]), (0, 0)))

    def idx_stack(a):
        w = a.shape[-1] // H_IDX
        a = jnp.transpose(to_seq(a).reshape(nb, nt, H_IDX, w), (0, 2, 1, 3))
        a = jnp.pad(a, ((0, 0), (0, 0), (0, SUBLANES - nt), (0, 0)))
        return a.reshape(nb, H_IDX * SUBLANES, w)

    yp = x_prompt
    ys = to_rows(x_sample)
    pk, pv, pki, pca, pckv, pkpe, pff = [], [], [], [], [], [], []
    sk, sv, ski, sca, sckv, skpe, sgv, sff = [], [], [], [], [], [], [], []

    for layer in range(depth):
        if layer % 2 == 0:
            e = layer // 2
            w, nrm = _even_weights(w_in_e[e], q_norm_b[e], k_norm_b[e])
            tabs_p = _rope_tables(pos_p, ROT_B, DH_B)
            tabs_s = _rope_tables(pos_s, ROT_B, DH_B)
            ap, qp, kp, vp, kip, wip, qihp, qilp, kbp, vbp, kihp, kilp = _even_proj(
                yp, norm_mix[layer], w, nrm, tabs_p, PROMPT_ROWS)
            a_s, qs, ks, vs, kis, wis, qihs, qils, kbs, vbs, kihs, kils = _even_proj(
                ys, norm_mix[layer], w, nrm, tabs_s, srows)

            hist = CONV_A_WIDTH - 1
            conv = (conv_a_w[e], conv_a_b[e], conv_a_ln_g[e], conv_a_ln_b[e])
            cp = _conv_tail(ap, jnp.zeros((bsz, 32, A_CH), F32), *conv, CONV_ROWS, 1, 32)
            cs = _conv_tail(a_s, to_rows(state_conv_a[e]), *conv, srows, nb, nb)
            pca.append(ap[:, seq - hist:, :])
            sca.append(jnp.concatenate([state_conv_a[e], to_seq(a_s)], axis=1)[:, nt:, :])

            bp = _dsa_prompt(qp, qihp, qilp, wip, kbp, vbp, kihp, kilp, DSA_QBLOCK)

            qh32, ql32 = idx_stack(qihs), idx_stack(qils)
            wi32 = jnp.broadcast_to(idx_stack(wis[..., :H_IDX]), (nb, H_IDX * SUBLANES, LANES))
            n_pool = cache_dsa_k.shape[1]
            kv_t = lambda c: jnp.transpose(c, (0, 1, 3, 4, 2)).reshape(-1, n_pool, HKV_B * DH_B, PAGE_SIZE)
            new_t = lambda a: jnp.transpose(pad_rows(to_seq(a), PAGE_SIZE), (0, 2, 1))
            scores = _dsa_sample_scores(page_table, qh32, ql32, wi32, jnp.transpose(cache_dsa_kidx, (0, 1, 3, 2)), e)
            bs8 = _dsa_sample_attend(
                page_table, scores, qh32, ql32, wi32, new_t(kihs[..., :D_IDX]), new_t(kils[..., :D_IDX]),
                pad_rows(to_seq(qs), SUBLANES), new_t(kbs), new_t(vbs), kv_t(cache_dsa_k), kv_t(cache_dsa_v), e, nt)
            bs = to_rows(bs8[:, :nt])

            wa = w_out_e[e][:A_CH].astype(BF)
            wb = w_out_e[e][A_CH:][_DSA_HEAD_PERM].astype(BF)
            yp = _out_proj(yp, cp, bp, wa, wb, PROMPT_ROWS)
            ys = _out_proj(ys, cs, bs, wa, wb, srows)

            pk.append(kp.reshape(bsz, seq, HKV_B, DH_B))
            pv.append(vp.reshape(bsz, seq, HKV_B, DH_B))
            pki.append(kip)
            sk.append(to_seq(ks).reshape(nb, nt, HKV_B, DH_B))
            sv.append(to_seq(vs).reshape(nb, nt, HKV_B, DH_B))
            ski.append(to_seq(kis))
        else:
            o = layer // 2
            w, wqb, wukp, wuk, wuv, qn, kn = _odd_weights(w_in_o[o], w_qb[o], w_uk[o], w_uv[o],
                                                          q_norm_c[o], k_norm_c[o])
            tabs_p = _rope_tables(pos_p, ROPE_C, LANES)
            tabs_s = _rope_tables(pos_s, ROPE_C, LANES)
            rest = (w, wqb, wukp, wuv, q_a_norm[o], kv_a_norm[o], qn, kn, gmlp_ln_g[o], gmlp_ln_b[o])
            ckvp, kpep, up, vp, q_p, k_p, v_p = _odd_proj(yp, norm_mix[layer], *rest, tabs_p, PROMPT_ROWS)
            ckvs, kpes, us, vs, q_s, k_s, v_s = _odd_proj(ys, norm_mix[layer], *rest, tabs_s, srows)

            mp = _mla_prompt(q_p, k_p, v_p, MLA_QBLOCK)
            gp = _gmlp_prompt(up, vp, w_spatial[o], b_spatial[o], PROMPT_ROWS)
            gs = _gmlp_sample(us[0], vs[0], w_spatial[o], b_spatial[o], nb, nt)[None]

            half = ROPE_C // 2
            inv_freq = jnp.power(jnp.float32(ROPE_THETA), -jnp.arange(half, dtype=F32) * (2.0 / ROPE_C))
            ang = jnp.arange(past).astype(F32)[:, None] * inv_freq[None, :]
            ctab = jnp.tile(jnp.cos(ang), (1, 2)).T
            stab = jnp.tile(jnp.sin(ang), (1, 2)).T
            ms8 = _mla_sample(page_table, pad_rows(to_seq(q_s), SUBLANES), pad_rows(to_seq(k_s), PAGE_SIZE),
                              pad_rows(to_seq(v_s), PAGE_SIZE), wukp, wuk, wuv, kn, ctab, stab,
                              cache_mla_ckv, jnp.transpose(cache_mla_kpe, (0, 1, 3, 2)), o, nt)
            ms = to_rows(ms8[:, :nt])

            wa = w_out_o[o][:HC * VD_C].astype(BF)
            wb = w_out_o[o][HC * VD_C:].astype(BF)
            yp = _out_proj(yp, mp, gp, wa, wb, PROMPT_ROWS)
            ys = _out_proj(ys, ms, gs, wa, wb, srows)

            pckv.append(ckvp)
            pkpe.append(kpep)
            sckv.append(to_seq(ckvs))
            skpe.append(to_seq(kpes))
            sgv.append(to_seq(vs))

        keep = FFN_CONV_WIDTH - 1
        ffn = (norm_ffn[layer], w_up[layer].astype(BF), ffn_conv_w[layer], ffn_conv_b[layer],
               w_down[layer].astype(BF))
        yp, tail_p = _ffn(yp, *ffn, jnp.zeros((bsz, SUBLANES, 2 * D_FF), F32), PROMPT_ROWS, 1)
        ys, tail_s = _ffn(ys, *ffn, to_rows(state_ffn_conv[layer]), srows, nb)
        pff.append(tail_p[:, SUBLANES - keep:, :])
        sff.append(to_seq(tail_s))

    return (yp, to_seq(ys),
            jnp.stack(pk), jnp.stack(pv), jnp.stack(pki), jnp.stack(pca),
            jnp.stack(pckv), jnp.stack(pkpe), jnp.stack(pff),
            jnp.stack(sk), jnp.stack(sv), jnp.stack(ski), jnp.stack(sca),
            jnp.stack(sckv), jnp.stack(skpe), jnp.stack(sgv), jnp.stack(sff))
```

```python
import functools

import numpy as np
import jax
import jax.numpy as jnp
from jax import lax
from jax.experimental import pallas as pl
from jax.experimental.pallas import tpu as pltpu

F32 = jnp.float32
BF = jnp.bfloat16

D_MODEL = 1024
PAGE_SIZE = 128
ROPE_THETA = 500000.0
EPS = 1e-6

A_CH = D_MODEL // 2
CONV_A_WIDTH = 31

DH_B = 64
HB = (D_MODEL // 2) // DH_B
HKV_B = 2
ROT_B = DH_B // 4
H_IDX = 4
D_IDX = 64
ROT_IDX = D_IDX // 4
TOPK_MAX = 256
IDX_W_SCALE = (H_IDX * D_IDX) ** -0.5

VD_C = 64
HC = (D_MODEL // 2) // VD_C
NOPE_C = 64
ROPE_C = 32
DQK_C = NOPE_C + ROPE_C
Q_LORA = 3 * D_MODEL // 8
KV_LORA = D_MODEL // 4
MLA_SCALE = DQK_C ** -0.5

D_CH = D_MODEL // 2
D_GROUPS = 8
CHUNK = 128

D_FF = 11 * D_MODEL // 4
FFN_CONV_WIDTH = 3

LANES = 128
SUBLANES = 8
NEG = -1e30
VMEM_LIMIT = 56 * 1024 * 1024

E_A, E_G, E_Q, E_K, E_V, E_QI, E_KI, E_WI, E_END = 0, 512, 1024, 1536, 1664, 1792, 2048, 2176, 2304
O_QA, O_CKV, O_KPE, O_U, O_V, O_END = 0, 384, 640, 768, 1280, 1792


def _dot(a, b):
    return jnp.dot(a, b, preferred_element_type=F32)


def _dot_nt(a, b):
    return lax.dot_general(a, b, (((1,), (1,)), ((), ())), preferred_element_type=F32)


def _split2(x):
    hi = x.astype(BF)
    lo = (x - hi.astype(F32)).astype(BF)
    return hi, lo


def _dot2(x, m):
    hi, lo = _split2(x)
    return _dot(hi, m) + _dot(lo, m)


def _rms(x, g):
    return x * lax.rsqrt(jnp.mean(x * x, axis=-1, keepdims=True) + EPS) * g


def _rope128(x, c, sa, sb, half):
    return x * c + pltpu.roll(x, LANES - half, 1) * sa + pltpu.roll(x, half, 1) * sb


def _params(sem, vmem=VMEM_LIMIT):
    return pltpu.CompilerParams(dimension_semantics=sem, vmem_limit_bytes=vmem)


def _full(shape):
    n = len(shape)
    return pl.BlockSpec(shape, lambda *_: (0,) * n)


def _low_half(rows):
    return lax.broadcasted_iota(jnp.int32, (rows, LANES), 1) < (LANES // 2)


def _rope_tables(pos, n_rot, head_w):
    half = n_rot // 2
    inv_freq = jnp.power(jnp.float32(ROPE_THETA), -jnp.arange(half, dtype=F32) * (2.0 / n_rot))
    ang = pos.astype(F32)[:, None] * inv_freq[None, :]
    cos, sin = jnp.cos(ang), jnp.sin(ang)
    m = pos.shape[0]
    one = jnp.ones((m, head_w - n_rot), F32)
    zero = jnp.zeros((m, head_w - n_rot), F32)
    zh = jnp.zeros((m, half), F32)
    c = jnp.concatenate([cos, cos, one], axis=1)
    sa = jnp.concatenate([-sin, zh, zero], axis=1)
    sb = jnp.concatenate([zh, sin, zero], axis=1)
    rep = LANES // head_w
    return tuple(jnp.tile(t, (1, rep)) for t in (c, sa, sb))


def _indicator(width, group):
    lane = jnp.arange(width)[:, None] // group
    col = jnp.arange(LANES)[None, :]
    ind = (lane == col).astype(BF)
    return ind, ind.T


def _even_proj_body(x_ref, g_ref, w_ref, nrm_ref, ind_ref, indt_ref, c_ref, sa_ref, sb_ref,
                    a_ref, q_ref, k_ref, v_ref, ki_ref, wi_ref, qih_ref, qil_ref, kb_ref, vb_ref, kih_ref, kil_ref):
    h = _rms(x_ref[0], g_ref[...]).astype(BF)
    z = _dot(h, w_ref[...])
    a_ref[0] = z[:, E_A:E_G] * jax.nn.sigmoid(z[:, E_G:E_Q])
    c, sa, sb = c_ref[...], sa_ref[...], sb_ref[...]
    half = ROT_B // 2
    qk = z[:, E_Q:E_V]
    ssq = _dot2(qk * qk, ind_ref[...])
    r = lax.rsqrt(ssq / DH_B + EPS)
    qk = qk * _dot2(r, indt_ref[...]) * nrm_ref[...]
    for s in range(4):
        slab = _rope128(qk[:, s * LANES:(s + 1) * LANES], c, sa, sb, half)
        q_ref[0, :, s * LANES:(s + 1) * LANES] = (slab * (DH_B ** -0.5)).astype(BF)
    k = _rope128(qk[:, 4 * LANES:5 * LANES], c, sa, sb, half)
    k_ref[0] = k
    kb_ref[0] = k.astype(BF)
    v = z[:, E_V:E_QI]
    v_ref[0] = v
    vb_ref[0] = v.astype(BF)
    for s in range(2):
        lo = E_QI + s * LANES
        hi, lw = _split2(_rope128(z[:, lo:lo + LANES], c, sa, sb, half))
        qih_ref[0, :, s * LANES:(s + 1) * LANES] = hi
        qil_ref[0, :, s * LANES:(s + 1) * LANES] = lw
    ki = _rope128(z[:, E_KI:E_WI], c, sa, sb, half)
    ki_ref[0] = ki[:, :D_IDX]
    kih_ref[0], kil_ref[0] = _split2(ki)
    wi_ref[0] = z[:, E_WI:E_END] * IDX_W_SCALE


def _even_weights(w_in, q_norm, k_norm):
    cuts = [0, 2 * A_CH, 2 * A_CH + 512, 2 * A_CH + 640, 2 * A_CH + 768, 2 * A_CH + 1024, 2 * A_CH + 1088]
    glu = w_in[:, cuts[0]:cuts[1]]
    q = w_in[:, cuts[1]:cuts[2]]
    k = w_in[:, cuts[2]:cuts[3]]
    v = w_in[:, cuts[3]:cuts[4]]
    qi = w_in[:, cuts[4]:cuts[5]]
    ki = w_in[:, cuts[5]:cuts[6]]
    wi = w_in[:, cuts[6]:]
    q = q[:, _DSA_HEAD_PERM]
    pad = jnp.zeros((w_in.shape[0], LANES - H_IDX), w_in.dtype)
    w = jnp.concatenate([glu, q, k, v, qi, ki, ki, wi, pad], axis=1).astype(BF)
    nrm = jnp.concatenate([jnp.tile(q_norm, HB), jnp.tile(k_norm, HKV_B)])[None, :]
    return w, nrm


def _dsa_head_perm():
    j = np.arange(HB * DH_B)
    head = (j // LANES) + (HB // HKV_B) * ((j % LANES) // DH_B)
    return head * DH_B + (j % DH_B)


_DSA_HEAD_PERM = _dsa_head_perm()


def _even_proj(x, g, w, nrm, tables, tb):
    grp, t, _ = x.shape
    ind, indt = _indicator(5 * LANES, DH_B)
    blk = lambda width: pl.BlockSpec((1, tb, width), lambda i, j: (i, j, 0))
    tab = pl.BlockSpec((tb, LANES), lambda i, j: (j, 0))
    sds = lambda width, dt: jax.ShapeDtypeStruct((grp, t, width), dt)
    return pl.pallas_call(
        _even_proj_body,
        grid=(grp, t // tb),
        in_specs=[blk(D_MODEL), _full((1, D_MODEL)), _full(w.shape), _full(nrm.shape),
                  _full(ind.shape), _full(indt.shape), tab, tab, tab],
        out_specs=[blk(A_CH), blk(512), blk(LANES), blk(LANES), blk(D_IDX), blk(LANES),
                   blk(256), blk(256), blk(LANES), blk(LANES), blk(LANES), blk(LANES)],
        out_shape=[sds(A_CH, F32), sds(512, BF), sds(LANES, F32), sds(LANES, F32), sds(D_IDX, F32),
                   sds(LANES, F32), sds(256, BF), sds(256, BF), sds(LANES, BF), sds(LANES, BF),
                   sds(LANES, BF), sds(LANES, BF)],
        compiler_params=_params(("parallel", "parallel")),
        name="even_proj",
    )(x, g[None, :], w, nrm, ind, indt, *tables)


def _conv_tail_body(a_ref, prev_ref, w_ref, b_ref, g_ref, beta_ref, o_ref, buf, *, tb, pad, stride, rows, carry):
    j = pl.program_id(1)

    @pl.when(j == 0)
    def _():
        buf[0:pad] = prev_ref[0]

    buf[pad:pad + tb] = a_ref[0]
    base = pad - (CONV_A_WIDTH - 1) * stride
    for r0 in range(0, tb, rows):
        acc = jnp.broadcast_to(b_ref[...], (rows, A_CH))
        for tap in range(CONV_A_WIDTH):
            acc = acc + w_ref[tap:tap + 1, :] * buf[pl.ds(base + tap * stride + r0, rows), :]
        mu = jnp.mean(acc, axis=-1, keepdims=True)
        d = acc - mu
        var = jnp.mean(d * d, axis=-1, keepdims=True)
        y = d * lax.rsqrt(var + EPS) * g_ref[...] + beta_ref[...]
        o_ref[0, r0:r0 + rows, :] = (y * jax.nn.sigmoid(y)).astype(BF)
    if carry:
        buf[0:pad] = buf[tb:tb + pad]


def _conv_tail(a, prev, w, b, g, beta, tb, stride, rows):
    grp, t, _ = a.shape
    pad = prev.shape[1]
    wp = jnp.concatenate([w, jnp.zeros((1, A_CH), w.dtype)], axis=0)
    body = functools.partial(_conv_tail_body, tb=tb, pad=pad, stride=stride, rows=rows, carry=t > tb)
    return pl.pallas_call(
        body,
        grid=(grp, t // tb),
        in_specs=[pl.BlockSpec((1, tb, A_CH), lambda i, j: (i, j, 0)),
                  pl.BlockSpec((1, pad, A_CH), lambda i, j: (i, 0, 0)),
                  _full(wp.shape), _full((1, A_CH)), _full((1, A_CH)), _full((1, A_CH))],
        out_specs=pl.BlockSpec((1, tb, A_CH), lambda i, j: (i, j, 0)),
        out_shape=jax.ShapeDtypeStruct((grp, t, A_CH), BF),
        scratch_shapes=[pltpu.VMEM((pad + tb, A_CH), F32)],
        compiler_params=_params(("parallel", "arbitrary")),
        name="conv_tail",
    )(a, prev, wp, b[None, :], g[None, :], beta[None, :])


def _sort_keys(score):
    score = jnp.where(score == 0.0, 0.0, score)
    bits = pltpu.bitcast(score, jnp.int32)
    return jnp.where(bits < 0, bits ^ jnp.int32(0x7FFFFFFF), bits)


def _kth_largest(key_ref, width, nsel):
    rows = key_ref.shape[0]
    int_min = jnp.int32(-2 ** 31)

    def step(it, thr):
        cand = thr + lax.shift_left(jnp.int32(1), 31 - it)
        cnt = jnp.sum(jnp.where(key_ref[:, 0:width] >= cand, 1.0, 0.0), axis=-1, keepdims=True)
        return jnp.where(cnt >= nsel, cand, thr)

    return lax.fori_loop(0, 32, step, jnp.full((rows, 1), int_min, jnp.int32), unroll=2)


def _select_bias(key_ref, bias_ref, valid, tri_ref, width, nsel, chunk, row_ok=None):
    thr = _kth_largest(key_ref, width, nsel)
    key = key_ref[:, 0:width]
    cnt_gt = jnp.sum(jnp.where(key > thr, 1.0, 0.0), axis=-1, keepdims=True)
    cnt_eq = jnp.sum(jnp.where(key == thr, 1.0, 0.0), axis=-1, keepdims=True)
    need = nsel - cnt_gt
    bias_ref[:, 0:width] = jnp.where(jnp.logical_and(key >= thr, valid), 0.0, NEG)

    surplus = cnt_eq - need
    if row_ok is not None:
        surplus = jnp.where(row_ok, surplus, 0.0)

    @pl.when(jnp.max(surplus) > 0.0)
    def _():
        off = jnp.zeros_like(need)
        for c0 in range(0, width, chunk):
            kc = key_ref[:, c0:c0 + chunk]
            eq = kc == thr
            eqf = jnp.where(eq, 1.0, 0.0)
            rank = _dot(eqf.astype(BF), tri_ref[...]) + off
            take = jnp.logical_or(kc > thr, jnp.logical_and(eq, rank < need))
            bias_ref[:, c0:c0 + chunk] = jnp.where(jnp.logical_and(take, valid[:, c0:c0 + chunk]), 0.0, NEG)
            off = off + jnp.sum(eqf, axis=-1, keepdims=True)


def _strict_upper(n):
    return (jnp.arange(n)[:, None] < jnp.arange(n)[None, :]).astype(BF)


def _stack_heads(x2, low, keep_low):
    zero = jnp.zeros_like(x2)
    return jnp.where(low, x2, zero) if keep_low else jnp.where(low, zero, x2)


DSA_STACK_ROWS = 512


def _dsa_prompt_block(q_ref, qih_ref, qil_ref, wi_ref, k_ref, v_ref, kih_ref, kil_ref, tri_ref, o_ref,
                      key_ref, bias_ref, *, tq, sx, nsel):
    i = pl.program_id(1)
    low = _low_half(tq)
    qpos = i * tq + lax.broadcasted_iota(jnp.int32, (tq, sx), 0)
    kpos = lax.broadcasted_iota(jnp.int32, (tq, sx), 1)
    valid = kpos <= qpos
    per_dot = max(1, DSA_STACK_ROWS // tq)

    def stack(ref, heads, pick_low):
        return jnp.concatenate([_stack_heads(ref[0, :, (h // 2) * LANES:(h // 2 + 1) * LANES], low, pick_low(h))
                                for h in heads], axis=0)

    wi = wi_ref[0]
    score = jnp.zeros((tq, sx), F32)
    for h0 in range(0, H_IDX, per_dot):
        heads = range(h0, min(H_IDX, h0 + per_dot))
        hi = stack(qih_ref, heads, lambda h: h % 2 == 0)
        lw = stack(qil_ref, heads, lambda h: h % 2 == 0)
        logits = (_dot_nt(hi, kih_ref[0, 0:sx, :]) + _dot_nt(hi, kil_ref[0, 0:sx, :])
                  + _dot_nt(lw, kih_ref[0, 0:sx, :]))
        for n, h in enumerate(heads):
            score = score + wi[:, h:h + 1] * jnp.maximum(logits[n * tq:(n + 1) * tq], 0.0)
    score = jnp.where(valid, score, -jnp.inf)
    key_ref[:, 0:sx] = _sort_keys(score)
    _select_bias(key_ref, bias_ref, valid, tri_ref, sx, nsel, LANES)

    bias = bias_ref[:, 0:sx]
    n_pairs = HB // HKV_B
    for p0 in range(0, n_pairs, per_dot):
        pairs = range(p0, min(n_pairs, p0 + per_dot))
        biasn = jnp.concatenate([bias] * len(pairs), axis=0)
        outs = []
        for g in range(HKV_B):
            qs = stack(q_ref, [2 * p for p in pairs], lambda h: g == 0)
            s = _dot_nt(qs, k_ref[0, 0:sx, :]) + biasn
            e = jnp.exp(s - jnp.max(s, axis=-1, keepdims=True))
            l = jnp.sum(e, axis=-1, keepdims=True)
            outs.append(_dot(e.astype(BF), v_ref[0, 0:sx, :]) / l)
        for n, p in enumerate(pairs):
            o_ref[0, :, p * LANES:(p + 1) * LANES] = jnp.where(
                low, outs[0][n * tq:(n + 1) * tq], outs[1][n * tq:(n + 1) * tq]).astype(BF)


def _causal_extents(n_blocks, tq, s):
    nb = 4 if n_blocks % 4 == 0 else 1
    per = n_blocks // nb
    return per, [min(s, (c + 1) * per * tq) for c in range(nb)]


def _dsa_prompt_body(*refs, tq, s, nsel):
    per, extents = _causal_extents(s // tq, tq, s)
    i = pl.program_id(1)
    for c, sx in enumerate(extents):
        pl.when(i // per == c)(functools.partial(_dsa_prompt_block, *refs, tq=tq, sx=sx, nsel=nsel))


def _dsa_prompt(q, qih, qil, wi, kb, vb, kih, kil, tq):
    n, s, _ = q.shape
    nsel = min(TOPK_MAX, s // 4)
    tri = _strict_upper(LANES)
    qblk = lambda width: pl.BlockSpec((1, tq, width), lambda i, j: (i, j, 0))
    sblk = lambda width: pl.BlockSpec((1, s, width), lambda i, j: (i, 0, 0))
    return pl.pallas_call(
        functools.partial(_dsa_prompt_body, tq=tq, s=s, nsel=nsel),
        grid=(n, s // tq),
        in_specs=[qblk(512), qblk(256), qblk(256), qblk(LANES), sblk(LANES), sblk(LANES), sblk(LANES),
                  sblk(LANES), _full(tri.shape)],
        out_specs=qblk(512),
        out_shape=jax.ShapeDtypeStruct((n, s, 512), BF),
        scratch_shapes=[pltpu.VMEM((tq, s), jnp.int32), pltpu.VMEM((tq, s), F32)],
        compiler_params=_params(("parallel", "parallel")),
        name="dsa_prompt",
    )(q, qih, qil, wi, kb, vb, kih, kil, tri)


def _out_proj_body(x_ref, a_ref, b_ref, wa_ref, wb_ref, o_ref):
    o_ref[0] = x_ref[0] + _dot(a_ref[0], wa_ref[...]) + _dot(b_ref[0], wb_ref[...])


def _out_proj(x, a, b, wa, wb, tb):
    grp, t, _ = x.shape
    blk = lambda width: pl.BlockSpec((1, tb, width), lambda i, j: (i, j, 0))
    return pl.pallas_call(
        _out_proj_body,
        grid=(grp, t // tb),
        in_specs=[blk(D_MODEL), blk(a.shape[-1]), blk(b.shape[-1]), _full(wa.shape), _full(wb.shape)],
        out_specs=blk(D_MODEL),
        out_shape=jax.ShapeDtypeStruct(x.shape, F32),
        compiler_params=_params(("parallel", "parallel")),
        name="out_proj",
    )(x, a, b, wa, wb)


FFN_TILE = 256


def _ffn_body(x_ref, g_ref, wu_ref, cw_ref, cb_ref, wd_ref, prev_ref, o_ref, tail_ref,
              hbuf, carry, ubuf, act, *, tb, pad, stride):
    j = pl.program_id(1)
    hbuf[...] = _rms(x_ref[0], g_ref[...]).astype(BF)

    @pl.when(j == 0)
    def _():
        carry[...] = prev_ref[0]

    for f in range(D_FF // FFN_TILE):
        halves = []
        for part in range(2):
            c0 = part * D_FF + f * FFN_TILE
            u = _dot(hbuf[...], wu_ref[:, c0:c0 + FFN_TILE])
            ubuf[0:pad] = carry[:, c0:c0 + FFN_TILE]
            ubuf[pad:pad + tb] = u
            y = (cw_ref[0:1, c0:c0 + FFN_TILE] * ubuf[pl.ds(pad - 2 * stride, tb), :]
                 + cw_ref[1:2, c0:c0 + FFN_TILE] * ubuf[pl.ds(pad - stride, tb), :]
                 + cw_ref[2:3, c0:c0 + FFN_TILE] * u + cb_ref[:, c0:c0 + FFN_TILE])
            carry[:, c0:c0 + FFN_TILE] = ubuf[tb:tb + pad]
            halves.append(y)
        act[:, f * FFN_TILE:(f + 1) * FFN_TILE] = (halves[1] * jax.nn.sigmoid(halves[1]) * halves[0]).astype(BF)
    o_ref[0] = x_ref[0] + _dot(act[...], wd_ref[...])

    @pl.when(j == pl.num_programs(1) - 1)
    def _():
        tail_ref[0] = carry[...]


def _ffn(x, g, wu, cw, cb, wd, prev, tb, stride):
    grp, t, _ = x.shape
    pad = prev.shape[1]
    cwp = jnp.concatenate([cw, jnp.zeros((SUBLANES - FFN_CONV_WIDTH, 2 * D_FF), cw.dtype)], axis=0)
    blk = pl.BlockSpec((1, tb, D_MODEL), lambda i, j: (i, j, 0))
    pblk = pl.BlockSpec((1, pad, 2 * D_FF), lambda i, j: (i, 0, 0))
    once = lambda shape: pl.BlockSpec(shape, lambda i, j: (0,) * len(shape), pipeline_mode=pl.Buffered(1))
    return pl.pallas_call(
        functools.partial(_ffn_body, tb=tb, pad=pad, stride=stride),
        grid=(grp, t // tb),
        in_specs=[blk, _full((1, D_MODEL)), once(wu.shape), _full(cwp.shape), _full((1, 2 * D_FF)),
                  once(wd.shape), pblk],
        out_specs=[blk, pblk],
        out_shape=[jax.ShapeDtypeStruct(x.shape, F32), jax.ShapeDtypeStruct(prev.shape, F32)],
        scratch_shapes=[pltpu.VMEM((tb, D_MODEL), BF), pltpu.VMEM((pad, 2 * D_FF), F32),
                        pltpu.VMEM((pad + tb, FFN_TILE), F32), pltpu.VMEM((tb, D_FF), BF)],
        compiler_params=_params(("parallel", "arbitrary")),
        name="conv_ffn",
    )(x, g[None, :], wu, cwp, cb[None, :], wd, prev)


def _head_norm128(x, gain, ind_ref, indt_ref, dim):
    ssq = _dot2(x * x, ind_ref[...])
    r = lax.rsqrt(ssq / dim + EPS)
    return x * _dot2(r, indt_ref[...]) * gain


def _odd_proj_body(x_ref, g_ref, w_ref, qan_ref, wqb_ref, kvn_ref, qn_ref, kn_ref, ind_ref, indt_ref,
                   c_ref, sa_ref, sb_ref, glg_ref, glb_ref, wuk_ref, wuv_ref,
                   ckv_ref, kpe_ref, u_ref, v_ref, q_ref, k_ref, vv_ref):
    h = _rms(x_ref[0], g_ref[...]).astype(BF)
    z = _dot(h, w_ref[...])
    c, sa, sb = c_ref[...], sa_ref[...], sb_ref[...]
    half = ROPE_C // 2

    qa = _rms(z[:, O_QA:O_CKV], qan_ref[...]).astype(BF)
    q = _head_norm128(_dot(qa, wqb_ref[...]), qn_ref[...], ind_ref, indt_ref, DQK_C)
    for hd in range(HC):
        sl = slice(hd * LANES, (hd + 1) * LANES)
        q_ref[0, :, sl] = (_rope128(q[:, sl], c, sa, sb, half) * MLA_SCALE).astype(BF)

    ckv = _rms(z[:, O_CKV:O_KPE], kvn_ref[...])
    ckv_ref[0] = ckv
    cb = ckv.astype(BF)
    kpe = z[:, O_KPE:O_U]
    kpe_ref[0] = kpe[:, :ROPE_C]
    kfull = _dot(cb, wuk_ref[...]) + jnp.concatenate([kpe] * HC, axis=1)
    k = _head_norm128(kfull, kn_ref[...], ind_ref, indt_ref, DQK_C)
    for hd in range(HC):
        sl = slice(hd * LANES, (hd + 1) * LANES)
        k_ref[0, :, sl] = _rope128(k[:, sl], c, sa, sb, half).astype(BF)
    vv_ref[0] = _dot(cb, wuv_ref[...]).astype(BF)

    zz = jax.nn.gelu(z[:, O_U:O_END])
    u_ref[0] = zz[:, :D_CH]
    vz = zz[:, D_CH:]
    mu = jnp.mean(vz, axis=-1, keepdims=True)
    d = vz - mu
    var = jnp.mean(d * d, axis=-1, keepdims=True)
    v_ref[0] = d * lax.rsqrt(var + EPS) * glg_ref[...] + glb_ref[...]


def _pad_heads(w, lead):
    z32 = jnp.zeros(lead + (HC, ROPE_C), w.dtype)
    return jnp.concatenate([z32, w, z32], axis=-1).reshape(lead + (HC * LANES,))


def _odd_weights(w_in, w_qb, w_uk, w_uv, q_norm, k_norm):
    d = w_in.shape[0]
    cuts = [Q_LORA, Q_LORA + KV_LORA, Q_LORA + KV_LORA + ROPE_C]
    pad = jnp.zeros((d, LANES - ROPE_C), w_in.dtype)
    w = jnp.concatenate([w_in[:, :cuts[2]], pad, w_in[:, cuts[2]:]], axis=1).astype(BF)
    qb = w_qb.reshape(Q_LORA, HC, DQK_C)
    qb = jnp.concatenate([qb, jnp.zeros((Q_LORA, HC, LANES - DQK_C), w_qb.dtype)], axis=-1)
    wqb = qb.reshape(Q_LORA, HC * LANES).astype(BF)
    wuk_pad = _pad_heads(w_uk, (KV_LORA,)).astype(BF)
    wuk = w_uk.reshape(KV_LORA, HC * NOPE_C).astype(BF)
    wuv = w_uv.reshape(KV_LORA, HC * VD_C).astype(BF)
    gain = lambda g: jnp.tile(jnp.concatenate([g, jnp.zeros((LANES - DQK_C,), g.dtype)]), HC)[None, :]
    return w, wqb, wuk_pad, wuk, wuv, gain(q_norm), gain(k_norm)


def _odd_proj(x, g, w, wqb, wuk_pad, wuv, qan, kvn, qn, kn, glg, glb, tables, tb):
    grp, t, _ = x.shape
    ind, indt = _indicator(HC * LANES, LANES)
    blk = lambda width: pl.BlockSpec((1, tb, width), lambda i, j: (i, j, 0))
    tab = pl.BlockSpec((tb, LANES), lambda i, j: (j, 0))
    sds = lambda width, dt: jax.ShapeDtypeStruct((grp, t, width), dt)
    row = lambda v: v[None, :]
    return pl.pallas_call(
        _odd_proj_body,
        grid=(grp, t // tb),
        in_specs=[blk(D_MODEL), _full((1, D_MODEL)), _full(w.shape), _full((1, Q_LORA)), _full(wqb.shape),
                  _full((1, KV_LORA)), _full(qn.shape), _full(kn.shape), _full(ind.shape), _full(indt.shape),
                  tab, tab, tab, _full((1, D_CH)), _full((1, D_CH)), _full(wuk_pad.shape), _full(wuv.shape)],
        out_specs=[blk(KV_LORA), blk(ROPE_C), blk(D_CH), blk(D_CH), blk(HC * LANES), blk(HC * LANES),
                   blk(HC * VD_C)],
        out_shape=[sds(KV_LORA, F32), sds(ROPE_C, F32), sds(D_CH, F32), sds(D_CH, F32),
                   sds(HC * LANES, BF), sds(HC * LANES, BF), sds(HC * VD_C, BF)],
        compiler_params=_params(("parallel", "parallel")),
        name="odd_proj",
    )(x, row(g), w, row(qan), wqb, row(kvn), qn, kn, ind, indt, *tables, row(glg), row(glb), wuk_pad, wuv)


def _gmlp_prompt_body(u_ref, v_ref, w_ref, b_ref, o_ref, *, tb):
    rows = D_GROUPS * CHUNK
    t_in = lax.broadcasted_iota(jnp.int32, (rows, CHUNK), 0) % CHUNK
    s_in = lax.broadcasted_iota(jnp.int32, (rows, CHUNK), 1)
    w = jnp.where(s_in <= t_in, w_ref[...], 0.0).astype(BF)
    grp = lax.broadcasted_iota(jnp.int32, (CHUNK, D_CH), 1) // (D_CH // D_GROUPS)
    for c0 in range(0, tb, CHUNK):
        y = _dot(w, v_ref[0, c0:c0 + CHUNK, :].astype(BF))
        mix = b_ref[...]
        for gi in range(D_GROUPS):
            mix = mix + jnp.where(grp == gi, y[gi * CHUNK:(gi + 1) * CHUNK], 0.0)
        o_ref[0, c0:c0 + CHUNK, :] = (u_ref[0, c0:c0 + CHUNK, :] * mix).astype(BF)


def _gmlp_prompt(u, v, w_s, b_s, tb):
    n, s, _ = u.shape
    w = w_s.reshape(D_GROUPS * CHUNK, CHUNK)
    b = jnp.repeat(b_s.T, D_CH // D_GROUPS, axis=1)
    blk = pl.BlockSpec((1, tb, D_CH), lambda i, j: (i, j, 0))
    return pl.pallas_call(
        functools.partial(_gmlp_prompt_body, tb=tb),
        grid=(n, s // tb),
        in_specs=[blk, blk, _full(w.shape), _full(b.shape)],
        out_specs=blk,
        out_shape=jax.ShapeDtypeStruct(u.shape, BF),
        compiler_params=_params(("parallel", "parallel")),
        name="gmlp_prompt",
    )(u, v, w, b)


def _gmlp_sample_body(u_ref, v_ref, w_ref, b_ref, o_ref, *, nb, nt):
    for t in range(nt):
        mix = jnp.broadcast_to(b_ref[t:t + 1, :], (nb, D_CH))
        for s in range(t + 1):
            mix = mix + w_ref[t * nt + s:t * nt + s + 1, :] * v_ref[s * nb:(s + 1) * nb, :]
        o_ref[t * nb:(t + 1) * nb, :] = (u_ref[t * nb:(t + 1) * nb, :] * mix).astype(BF)


def _gmlp_sample(u, v, w_s, b_s, nb, nt):
    lane = D_CH // D_GROUPS
    w = jnp.repeat(jnp.transpose(w_s[:, :nt, :nt], (1, 2, 0)).reshape(nt * nt, D_GROUPS), lane, axis=1)
    b = jnp.repeat(b_s.T[:nt], lane, axis=1)
    return pl.pallas_call(
        functools.partial(_gmlp_sample_body, nb=nb, nt=nt),
        out_shape=jax.ShapeDtypeStruct(u.shape, BF),
        name="gmlp_sample",
    )(u, v, w, b)


def _mla_prompt_block(q_ref, k_ref, v_ref, o_ref, *, tq, sx):
    i = pl.program_id(1)
    low = _low_half(tq)
    qpos = i * tq + lax.broadcasted_iota(jnp.int32, (tq, sx), 0)
    kpos = lax.broadcasted_iota(jnp.int32, (tq, sx), 1)
    bias = jnp.where(kpos <= qpos, 0.0, NEG)
    for p in range(HC // 2):
        outs = []
        for hd in (2 * p, 2 * p + 1):
            sl = slice(hd * LANES, (hd + 1) * LANES)
            s = _dot_nt(q_ref[0, :, sl], k_ref[0, 0:sx, sl]) + bias
            e = jnp.exp(s - jnp.max(s, axis=-1, keepdims=True))
            l = jnp.sum(e, axis=-1, keepdims=True)
            outs.append(_dot(e.astype(BF), v_ref[0, 0:sx, p * LANES:(p + 1) * LANES]) / l)
        o_ref[0, :, p * LANES:(p + 1) * LANES] = jnp.where(low, outs[0], outs[1]).astype(BF)


def _mla_prompt_body(*refs, tq, s):
    per, extents = _causal_extents(s // tq, tq, s)
    i = pl.program_id(1)
    for c, sx in enumerate(extents):
        pl.when(i // per == c)(functools.partial(_mla_prompt_block, *refs, tq=tq, sx=sx))


def _mla_prompt(q, k, v, tq):
    n, s, _ = q.shape
    return pl.pallas_call(
        functools.partial(_mla_prompt_body, tq=tq, s=s),
        grid=(n, s // tq),
        in_specs=[pl.BlockSpec((1, tq, HC * LANES), lambda i, j: (i, j, 0)),
                  pl.BlockSpec((1, s, HC * LANES), lambda i, j: (i, 0, 0)),
                  pl.BlockSpec((1, s, HC * VD_C), lambda i, j: (i, 0, 0))],
        out_specs=pl.BlockSpec((1, tq, HC * VD_C), lambda i, j: (i, j, 0)),
        out_shape=jax.ShapeDtypeStruct((n, s, HC * VD_C), BF),
        compiler_params=_params(("parallel", "parallel")),
        name="mla_prompt",
    )(q, k, v)


PAGES_PER_STEP = 32


def _page_specs(cache_shape, layer):
    _, _, rows, width = cache_shape

    def spec(slot):
        return pl.BlockSpec((None, None, rows, width),
                            lambda s, j, pt: (layer, pt[s, j * PAGES_PER_STEP + slot], 0, 0))

    return [spec(slot) for slot in range(PAGES_PER_STEP)]


def _cat_pages(refs, axis):
    return jnp.concatenate([r[...] for r in refs], axis=axis)


def _idx_scores(qh, ql, wi, kh, kl):
    both = _dot(jnp.concatenate([qh, ql], axis=0), kh)
    rows = qh.shape[0]
    logits = both[0:rows] + both[rows:2 * rows] + _dot(qh, kl)
    score = jnp.zeros((SUBLANES, logits.shape[1]), F32)
    for h in range(H_IDX):
        sl = slice(h * SUBLANES, (h + 1) * SUBLANES)
        score = score + wi[sl, 0:1] * jnp.maximum(logits[sl], 0.0)
    return score


def _dsa_sample_score_body(pt_ref, qh_ref, ql_ref, wi_ref, *rest):
    del pt_ref
    pages, o_ref = rest[:PAGES_PER_STEP], rest[PAGES_PER_STEP]
    kh, kl = _split2(_cat_pages(pages, 1))
    o_ref[0] = _idx_scores(qh_ref[0], ql_ref[0], wi_ref[0], kh, kl)


def _dsa_sample_scores(page_table, qh, ql, wi, cache_ki, layer):
    n, n_pages = page_table.shape
    steps = n_pages // PAGES_PER_STEP
    span = PAGES_PER_STEP * PAGE_SIZE
    per_seq = lambda shape: pl.BlockSpec((1,) + shape, lambda s, j, pt: (s, 0, 0))
    grid_spec = pltpu.PrefetchScalarGridSpec(
        num_scalar_prefetch=1,
        grid=(n, steps),
        in_specs=[per_seq((4 * SUBLANES, D_IDX)), per_seq((4 * SUBLANES, D_IDX)), per_seq((4 * SUBLANES, LANES))]
        + _page_specs(cache_ki.shape, layer),
        out_specs=pl.BlockSpec((1, SUBLANES, span), lambda s, j, pt: (s, 0, j)),
    )
    return pl.pallas_call(
        _dsa_sample_score_body,
        grid_spec=grid_spec,
        out_shape=jax.ShapeDtypeStruct((n, SUBLANES, n_pages * PAGE_SIZE), F32),
        compiler_params=_params(("parallel", "arbitrary")),
        name="dsa_sample_scores",
    )(page_table, qh, ql, wi, *([cache_ki] * PAGES_PER_STEP))


SELECT_SEQS = 8


def _dsa_sample_select_body(score_ref, qh_ref, ql_ref, wi_ref, knh_ref, knl_ref, tri_ref, bias_ref, key_ref,
                            *, past, nsel, nt):
    width = past + PAGE_SIZE
    rows = SELECT_SEQS * SUBLANES
    slot = lax.broadcasted_iota(jnp.int32, (rows, width), 0) % SUBLANES
    kpos = lax.broadcasted_iota(jnp.int32, (rows, width), 1)
    valid = kpos <= past + slot
    for g in range(SELECT_SEQS):
        sl = slice(g * SUBLANES, (g + 1) * SUBLANES)
        new = _idx_scores(qh_ref[g], ql_ref[g], wi_ref[g], knh_ref[g], knl_ref[g])
        key_ref[sl, 0:past] = _sort_keys(score_ref[g])
        key_ref[sl, past:width] = _sort_keys(jnp.where(valid[sl, past:width], new, -jnp.inf))
    row_ok = lax.broadcasted_iota(jnp.int32, (rows, 1), 0) % SUBLANES < nt
    _select_bias(key_ref, bias_ref, valid, tri_ref, width, nsel, LANES, row_ok)


def _dsa_sample_select(score, qh, ql, wi, knh, knl, nt):
    n, _, past = score.shape
    nsel = min(TOPK_MAX, (past + nt) // 4)
    tri = _strict_upper(LANES)
    width = past + PAGE_SIZE
    rows = SELECT_SEQS * SUBLANES
    blk = lambda a: pl.BlockSpec((SELECT_SEQS,) + a.shape[1:], lambda i: (i, 0, 0))
    return pl.pallas_call(
        functools.partial(_dsa_sample_select_body, past=past, nsel=nsel, nt=nt),
        grid=(n // SELECT_SEQS,),
        in_specs=[blk(score), blk(qh), blk(ql), blk(wi), blk(knh), blk(knl), _full(tri.shape)],
        out_specs=pl.BlockSpec((rows, width), lambda i: (i, 0)),
        out_shape=jax.ShapeDtypeStruct((n * SUBLANES, width), F32),
        scratch_shapes=[pltpu.VMEM((rows, width), jnp.int32)],
        compiler_params=_params(("parallel",)),
        name="dsa_sample_select",
    )(score, qh, ql, wi, knh, knl, tri)


def _dsa_sample_attend_body(pt_ref, bias_ref, btail_ref, q_ref, kn_ref, vn_ref, *rest):
    del pt_ref
    kpages = rest[:PAGES_PER_STEP]
    vpages = rest[PAGES_PER_STEP:2 * PAGES_PER_STEP]
    o_ref, q64_ref, m_ref, l_ref, acc_ref = rest[2 * PAGES_PER_STEP:]
    j = pl.program_id(1)
    rows8 = SUBLANES
    per = HB // HKV_B

    @pl.when(j == 0)
    def _():
        low = _low_half(rows8)
        q8 = q_ref[0].astype(F32)
        q64_ref[...] = jnp.concatenate(
            [_stack_heads(q8[:, (h % per) * LANES:(h % per + 1) * LANES], low, h < per) for h in range(HB)],
            axis=0).astype(BF)
        m_ref[...] = jnp.full(m_ref.shape, NEG, F32)
        l_ref[...] = jnp.zeros(l_ref.shape, F32)
        acc_ref[...] = jnp.zeros(acc_ref.shape, F32)

    def fold(bias8, k_t, v_t):
        s = _dot(q64_ref[...], k_t) + jnp.concatenate([bias8] * HB, axis=0)
        m_old = m_ref[...]
        m_new = jnp.maximum(m_old, jnp.max(s, axis=-1, keepdims=True))
        alpha = jnp.exp(m_old - m_new)
        e = jnp.exp(s - m_new)
        l_ref[...] = alpha * l_ref[...] + jnp.sum(e, axis=-1, keepdims=True)
        acc_ref[...] = alpha * acc_ref[...] + _dot_nt(e.astype(BF), v_t)
        m_ref[...] = m_new

    fold(bias_ref[0], _cat_pages(kpages, 1).astype(BF), _cat_pages(vpages, 1).astype(BF))

    @pl.when(j == pl.num_programs(1) - 1)
    def _():
        fold(btail_ref[0], kn_ref[0], vn_ref[0])
        out = acc_ref[...] / l_ref[...]
        low = _low_half(rows8)
        for p in range(per):
            o_ref[0, :, p * LANES:(p + 1) * LANES] = jnp.where(
                low, out[p * rows8:(p + 1) * rows8], out[(per + p) * rows8:(per + p + 1) * rows8]).astype(BF)


def _dsa_sample_attend(page_table, bias, q8, kn, vn, cache_k, cache_v, layer):
    n, n_pages = page_table.shape
    span = PAGES_PER_STEP * PAGE_SIZE
    per_seq = lambda shape: pl.BlockSpec((1,) + shape, lambda s, j, pt: (s, 0, 0))
    grid_spec = pltpu.PrefetchScalarGridSpec(
        num_scalar_prefetch=1,
        grid=(n, n_pages // PAGES_PER_STEP),
        in_specs=[pl.BlockSpec((1, SUBLANES, span), lambda s, j, pt: (s, 0, j)),
                  pl.BlockSpec((1, SUBLANES, PAGE_SIZE), lambda s, j, pt: (s, 0, n_pages)),
                  per_seq((SUBLANES, HB * DH_B)), per_seq((LANES, PAGE_SIZE)), per_seq((LANES, PAGE_SIZE))]
        + _page_specs(cache_k.shape, layer) + _page_specs(cache_v.shape, layer),
        out_specs=per_seq((SUBLANES, HB * DH_B)),
        scratch_shapes=[pltpu.VMEM((HB * SUBLANES, LANES), BF), pltpu.VMEM((HB * SUBLANES, 1), F32),
                        pltpu.VMEM((HB * SUBLANES, 1), F32), pltpu.VMEM((HB * SUBLANES, LANES), F32)],
    )
    return pl.pallas_call(
        _dsa_sample_attend_body,
        grid_spec=grid_spec,
        out_shape=jax.ShapeDtypeStruct((n, SUBLANES, HB * DH_B), BF),
        compiler_params=_params(("parallel", "arbitrary")),
        name="dsa_sample_attend",
    )(page_table, bias, bias, q8, kn, vn, *([cache_k] * PAGES_PER_STEP), *([cache_v] * PAGES_PER_STEP))


MLA_ROWS = HC * SUBLANES


def _mla_sample_body(pt_ref, q_ref, kn_ref, vn_ref, wukp_ref, wukt_ref, wuv_ref, gk_ref, ct_ref, st_ref,
                     *rest, nt):
    del pt_ref
    cpages = rest[:PAGES_PER_STEP]
    ppages = rest[PAGES_PER_STEP:2 * PAGES_PER_STEP]
    o_ref, qbd_ref, lhs_ref, qrot_ref, m_ref, l_ref, acc_ref = rest[2 * PAGES_PER_STEP:]
    j = pl.program_id(1)
    rows = MLA_ROWS
    feat = HC * NOPE_C

    @pl.when(j == 0)
    def _():
        q8 = q_ref[0].astype(F32)
        rhead = lax.broadcasted_iota(jnp.int32, (rows, HC * LANES), 0) // SUBLANES
        lhead = lax.broadcasted_iota(jnp.int32, (rows, HC * LANES), 1) // LANES
        qbd = jnp.where(rhead == lhead, jnp.concatenate([q8] * HC, axis=0), 0.0)
        qbd_ref[...] = qbd.astype(BF)
        lhs_ref[0:feat, :] = wukt_ref[...]
        lhs_ref[feat:feat + rows, :] = _dot_nt((qbd * gk_ref[...]).astype(BF), wukp_ref[...]).astype(BF)
        folded = qbd[:, 0:LANES]
        for hd in range(1, HC):
            folded = folded + qbd[:, hd * LANES:(hd + 1) * LANES]
        lane = lax.broadcasted_iota(jnp.int32, (rows, LANES), 1)
        half = ROPE_C // 2
        swapped = jnp.where(lane < half, pltpu.roll(folded, LANES - half, 1),
                            jnp.where(lane < ROPE_C, -pltpu.roll(folded, half, 1), 0.0))
        gpe = gk_ref[:, 0:LANES]
        qrot_ref[...] = jnp.concatenate([(folded * gpe)[:, 0:ROPE_C], (swapped * gpe)[:, 0:ROPE_C]],
                                        axis=1).astype(BF)
        m_ref[...] = jnp.full(m_ref.shape, NEG, F32)
        l_ref[...] = jnp.zeros(l_ref.shape, F32)
        acc_ref[...] = jnp.zeros(acc_ref.shape, F32)

    cb = _cat_pages(cpages, 0).astype(BF)
    kpe = _cat_pages(ppages, 1)
    big = _dot_nt(lhs_ref[...], cb)
    kp2 = jnp.sum(kpe * kpe, axis=0, keepdims=True)
    rot = _dot(qrot_ref[...], jnp.concatenate([kpe * ct_ref[...], kpe * st_ref[...]], axis=0).astype(BF))
    parts = []
    for hd in range(HC):
        kn = big[hd * NOPE_C:(hd + 1) * NOPE_C]
        r = lax.rsqrt((jnp.sum(kn * kn, axis=0, keepdims=True) + kp2) / DQK_C + EPS)
        sl = slice(hd * SUBLANES, (hd + 1) * SUBLANES)
        parts.append(r * (big[feat + hd * SUBLANES:feat + (hd + 1) * SUBLANES] + rot[sl]))
    s = jnp.concatenate(parts, axis=0)

    m_old = m_ref[...]
    m_new = jnp.maximum(m_old, jnp.max(s, axis=-1, keepdims=True))
    alpha = jnp.exp(m_old - m_new)
    e = jnp.exp(s - m_new)
    l_ref[...] = alpha * l_ref[...] + jnp.sum(e, axis=-1, keepdims=True)
    acc_ref[...] = alpha * acc_ref[...] + _dot(e.astype(BF), cb)
    m_ref[...] = m_new

    @pl.when(j == pl.num_programs(1) - 1)
    def _():
        s_new = _dot_nt(qbd_ref[...], kn_ref[0])
        t_row = lax.broadcasted_iota(jnp.int32, s_new.shape, 0) % SUBLANES
        col = lax.broadcasted_iota(jnp.int32, s_new.shape, 1)
        s_new = jnp.where(jnp.logical_and(col <= t_row, col < nt), s_new, NEG)
        m_old = m_ref[...]
        m_fin = jnp.maximum(m_old, jnp.max(s_new, axis=-1, keepdims=True))
        alpha = jnp.exp(m_old - m_fin)
        e = jnp.exp(s_new - m_fin)
        l = alpha * l_ref[...] + jnp.sum(e, axis=-1, keepdims=True)
        out = (_dot((alpha * acc_ref[...]).astype(BF), wuv_ref[...]) + _dot(e.astype(BF), vn_ref[0])) / l
        lhead = lax.broadcasted_iota(jnp.int32, (SUBLANES, HC * VD_C), 1) // VD_C
        res = jnp.zeros((SUBLANES, HC * VD_C), F32)
        for hd in range(HC):
            res = res + jnp.where(lhead == hd, out[hd * SUBLANES:(hd + 1) * SUBLANES], 0.0)
        o_ref[0] = res.astype(BF)


def _mla_sample(page_table, q8, kn, vn, wuk_pad, wuk, wuv, gk, ctab, stab, cache_ckv, cache_kpe, layer, nt):
    n, n_pages = page_table.shape
    span = PAGES_PER_STEP * PAGE_SIZE
    rows = MLA_ROWS
    wukt = wuk.T
    per_seq = lambda shape: pl.BlockSpec((1,) + shape, lambda s, j, pt: (s, 0, 0))
    const = lambda a: pl.BlockSpec(a.shape, lambda s, j, pt: (0,) * a.ndim)
    tab = pl.BlockSpec((ROPE_C, span), lambda s, j, pt: (0, j))
    grid_spec = pltpu.PrefetchScalarGridSpec(
        num_scalar_prefetch=1,
        grid=(n, n_pages // PAGES_PER_STEP),
        in_specs=[per_seq((SUBLANES, HC * LANES)), per_seq((PAGE_SIZE, HC * LANES)), per_seq((PAGE_SIZE, HC * VD_C)),
                  const(wuk_pad), const(wukt), const(wuv), const(gk), tab, tab]
        + _page_specs(cache_ckv.shape, layer) + _page_specs(cache_kpe.shape, layer),
        out_specs=per_seq((SUBLANES, HC * VD_C)),
        scratch_shapes=[pltpu.VMEM((rows, HC * LANES), BF), pltpu.VMEM((HC * NOPE_C + rows, KV_LORA), BF),
                        pltpu.VMEM((rows, 2 * ROPE_C), BF),
                        pltpu.VMEM((rows, 1), F32), pltpu.VMEM((rows, 1), F32), pltpu.VMEM((rows, KV_LORA), F32)],
    )
    return pl.pallas_call(
        functools.partial(_mla_sample_body, nt=nt),
        grid_spec=grid_spec,
        out_shape=jax.ShapeDtypeStruct((n, SUBLANES, HC * VD_C), BF),
        compiler_params=_params(("parallel", "arbitrary")),
        name="mla_sample",
    )(page_table, q8, kn, vn, wuk_pad, wukt, wuv, gk, ctab, stab,
      *([cache_ckv] * PAGES_PER_STEP), *([cache_kpe] * PAGES_PER_STEP))


PROMPT_ROWS = 512
CONV_ROWS = 256
DSA_QBLOCK = 256
MLA_QBLOCK = 256


def kernel(x_prompt, x_sample, cache_dsa_k, cache_dsa_v, cache_dsa_kidx, state_conv_a, cache_mla_ckv, cache_mla_kpe, state_ffn_conv, page_table, norm_mix, norm_ffn, w_in_e, conv_a_w, conv_a_b, conv_a_ln_g, conv_a_ln_b, q_norm_b, k_norm_b, w_out_e, w_in_o, q_a_norm, w_qb, kv_a_norm, w_uk, w_uv, q_norm_c, k_norm_c, gmlp_ln_g, gmlp_ln_b, w_spatial, b_spatial, w_out_o, w_up, ffn_conv_w, ffn_conv_b, w_down):
    bsz, seq, _ = x_prompt.shape
    nb, nt, _ = x_sample.shape
    depth = norm_mix.shape[0]
    past = page_table.shape[1] * PAGE_SIZE
    srows = nt * nb
    pos_p = jnp.arange(seq)
    pos_s = past + jnp.repeat(jnp.arange(nt), nb)

    def to_rows(a):
        return jnp.transpose(a, (1, 0, 2)).reshape(1, a.shape[1] * nb, a.shape[2])

    def to_seq(a):
        return jnp.transpose(a.reshape(-1, nb, a.shape[-1]), (1, 0, 2))

    def pad_rows(a, rows):
        return jnp.pad(a, ((0, 0), (0, rows - a.shape[1]), (0, 0)))

    def idx_stack(a):
        w = a.shape[-1] // H_IDX
        a = jnp.transpose(to_seq(a).reshape(nb, nt, H_IDX, w), (0, 2, 1, 3))
        a = jnp.pad(a, ((0, 0), (0, 0), (0, SUBLANES - nt), (0, 0)))
        return a.reshape(nb, H_IDX * SUBLANES, w)

    yp = x_prompt
    ys = to_rows(x_sample)
    pk, pv, pki, pca, pckv, pkpe, pff = [], [], [], [], [], [], []
    sk, sv, ski, sca, sckv, skpe, sgv, sff = [], [], [], [], [], [], [], []

    for layer in range(depth):
        if layer % 2 == 0:
            e = layer // 2
            w, nrm = _even_weights(w_in_e[e], q_norm_b[e], k_norm_b[e])
            tabs_p = _rope_tables(pos_p, ROT_B, DH_B)
            tabs_s = _rope_tables(pos_s, ROT_B, DH_B)
            ap, qp, kp, vp, kip, wip, qihp, qilp, kbp, vbp, kihp, kilp = _even_proj(
                yp, norm_mix[layer], w, nrm, tabs_p, PROMPT_ROWS)
            a_s, qs, ks, vs, kis, wis, qihs, qils, kbs, vbs, kihs, kils = _even_proj(
                ys, norm_mix[layer], w, nrm, tabs_s, srows)

            hist = CONV_A_WIDTH - 1
            conv = (conv_a_w[e], conv_a_b[e], conv_a_ln_g[e], conv_a_ln_b[e])
            cp = _conv_tail(ap, jnp.zeros((bsz, 32, A_CH), F32), *conv, CONV_ROWS, 1, 32)
            cs = _conv_tail(a_s, to_rows(state_conv_a[e]), *conv, srows, nb, nb)
            pca.append(ap[:, seq - hist:, :])
            sca.append(jnp.concatenate([state_conv_a[e], to_seq(a_s)], axis=1)[:, nt:, :])

            bp = _dsa_prompt(qp, qihp, qilp, wip, kbp, vbp, kihp, kilp, DSA_QBLOCK)

            qh32, ql32 = idx_stack(qihs), idx_stack(qils)
            wi32 = jnp.broadcast_to(idx_stack(wis[..., :H_IDX]), (nb, H_IDX * SUBLANES, LANES))
            n_pool = cache_dsa_k.shape[1]
            kv_t = lambda c: jnp.transpose(c, (0, 1, 3, 4, 2)).reshape(-1, n_pool, HKV_B * DH_B, PAGE_SIZE)
            new_t = lambda a: jnp.transpose(pad_rows(to_seq(a), PAGE_SIZE), (0, 2, 1))
            scores = _dsa_sample_scores(page_table, qh32, ql32, wi32, jnp.transpose(cache_dsa_kidx, (0, 1, 3, 2)), e)
            bias = _dsa_sample_select(scores, qh32, ql32, wi32, new_t(kihs[..., :D_IDX]), new_t(kils[..., :D_IDX]), nt)
            bs8 = _dsa_sample_attend(page_table, bias.reshape(nb, SUBLANES, -1), pad_rows(to_seq(qs), SUBLANES),
                                     new_t(kbs), new_t(vbs), kv_t(cache_dsa_k), kv_t(cache_dsa_v), e)
            bs = to_rows(bs8[:, :nt])

            wa = w_out_e[e][:A_CH].astype(BF)
            wb = w_out_e[e][A_CH:][_DSA_HEAD_PERM].astype(BF)
            yp = _out_proj(yp, cp, bp, wa, wb, PROMPT_ROWS)
            ys = _out_proj(ys, cs, bs, wa, wb, srows)

            pk.append(kp.reshape(bsz, seq, HKV_B, DH_B))
            pv.append(vp.reshape(bsz, seq, HKV_B, DH_B))
            pki.append(kip)
            sk.append(to_seq(ks).reshape(nb, nt, HKV_B, DH_B))
            sv.append(to_seq(vs).reshape(nb, nt, HKV_B, DH_B))
            ski.append(to_seq(kis))
        else:
            o = layer // 2
            w, wqb, wukp, wuk, wuv, qn, kn = _odd_weights(w_in_o[o], w_qb[o], w_uk[o], w_uv[o],
                                                          q_norm_c[o], k_norm_c[o])
            tabs_p = _rope_tables(pos_p, ROPE_C, LANES)
            tabs_s = _rope_tables(pos_s, ROPE_C, LANES)
            rest = (w, wqb, wukp, wuv, q_a_norm[o], kv_a_norm[o], qn, kn, gmlp_ln_g[o], gmlp_ln_b[o])
            ckvp, kpep, up, vp, q_p, k_p, v_p = _odd_proj(yp, norm_mix[layer], *rest, tabs_p, PROMPT_ROWS)
            ckvs, kpes, us, vs, q_s, k_s, v_s = _odd_proj(ys, norm_mix[layer], *rest, tabs_s, srows)

            mp = _mla_prompt(q_p, k_p, v_p, MLA_QBLOCK)
            gp = _gmlp_prompt(up, vp, w_spatial[o], b_spatial[o], PROMPT_ROWS)
            gs = _gmlp_sample(us[0], vs[0], w_spatial[o], b_spatial[o], nb, nt)[None]

            half = ROPE_C // 2
            inv_freq = jnp.power(jnp.float32(ROPE_THETA), -jnp.arange(half, dtype=F32) * (2.0 / ROPE_C))
            ang = jnp.arange(past).astype(F32)[:, None] * inv_freq[None, :]
            ctab = jnp.tile(jnp.cos(ang), (1, 2)).T
            stab = jnp.tile(jnp.sin(ang), (1, 2)).T
            ms8 = _mla_sample(page_table, pad_rows(to_seq(q_s), SUBLANES), pad_rows(to_seq(k_s), PAGE_SIZE),
                              pad_rows(to_seq(v_s), PAGE_SIZE), wukp, wuk, wuv, kn, ctab, stab,
                              cache_mla_ckv, jnp.transpose(cache_mla_kpe, (0, 1, 3, 2)), o, nt)
            ms = to_rows(ms8[:, :nt])

            wa = w_out_o[o][:HC * VD_C].astype(BF)
            wb = w_out_o[o][HC * VD_C:].astype(BF)
            yp = _out_proj(yp, mp, gp, wa, wb, PROMPT_ROWS)
            ys = _out_proj(ys, ms, gs, wa, wb, srows)

            pckv.append(ckvp)
            pkpe.append(kpep)
            sckv.append(to_seq(ckvs))
            skpe.append(to_seq(kpes))
            sgv.append(to_seq(vs))

        keep = FFN_CONV_WIDTH - 1
        ffn = (norm_ffn[layer], w_up[layer].astype(BF), ffn_conv_w[layer], ffn_conv_b[layer],
               w_down[layer].astype(BF))
        yp, tail_p = _ffn(yp, *ffn, jnp.zeros((bsz, SUBLANES, 2 * D_FF), F32), PROMPT_ROWS, 1)
        ys, tail_s = _ffn(ys, *ffn, to_rows(state_ffn_conv[layer]), srows, nb)
        pff.append(tail_p[:, SUBLANES - keep:, :])
        sff.append(to_seq(tail_s))

    return (yp, to_seq(ys),
            jnp.stack(pk), jnp.stack(pv), jnp.stack(pki), jnp.stack(pca),
            jnp.stack(pckv), jnp.stack(pkpe), jnp.stack(pff),
            jnp.stack(sk), jnp.stack(sv), jnp.stack(ski), jnp.stack(sca),
            jnp.stack(sckv), jnp.stack(skpe), jnp.stack(sgv), jnp.stack(sff))
```

```python
import functools

import numpy as np
import jax
import jax.numpy as jnp
from jax import lax
from jax.experimental import pallas as pl
from jax.experimental.pallas import tpu as pltpu

F32 = jnp.float32
BF = jnp.bfloat16

D_MODEL = 1024
PAGE_SIZE = 128
ROPE_THETA = 500000.0
EPS = 1e-6

A_CH = D_MODEL // 2
CONV_A_WIDTH = 31

DH_B = 64
HB = (D_MODEL // 2) // DH_B
HKV_B = 2
ROT_B = DH_B // 4
H_IDX = 4
D_IDX = 64
ROT_IDX = D_IDX // 4
TOPK_MAX = 256
IDX_W_SCALE = (H_IDX * D_IDX) ** -0.5

VD_C = 64
HC = (D_MODEL // 2) // VD_C
NOPE_C = 64
ROPE_C = 32
DQK_C = NOPE_C + ROPE_C
Q_LORA = 3 * D_MODEL // 8
KV_LORA = D_MODEL // 4
MLA_SCALE = DQK_C ** -0.5

D_CH = D_MODEL // 2
D_GROUPS = 8
CHUNK = 128

D_FF = 11 * D_MODEL // 4
FFN_CONV_WIDTH = 3

LANES = 128
SUBLANES = 8
NEG = -1e30
VMEM_LIMIT = 56 * 1024 * 1024

E_A, E_G, E_Q, E_K, E_V, E_QI, E_KI, E_WI, E_END = 0, 512, 1024, 1536, 1664, 1792, 2048, 2176, 2304
O_QA, O_CKV, O_KPE, O_U, O_V, O_END = 0, 384, 640, 768, 1280, 1792


def _dot(a, b):
    return jnp.dot(a, b, preferred_element_type=F32)


def _dot_nt(a, b):
    return lax.dot_general(a, b, (((1,), (1,)), ((), ())), preferred_element_type=F32)


def _split2(x):
    hi = x.astype(BF)
    lo = (x - hi.astype(F32)).astype(BF)
    return hi, lo


def _dot2(x, m):
    hi, lo = _split2(x)
    return _dot(hi, m) + _dot(lo, m)


def _rms(x, g):
    return x * lax.rsqrt(jnp.mean(x * x, axis=-1, keepdims=True) + EPS) * g


def _rope128(x, c, sa, sb, half):
    return x * c + pltpu.roll(x, LANES - half, 1) * sa + pltpu.roll(x, half, 1) * sb


def _params(sem, vmem=VMEM_LIMIT):
    return pltpu.CompilerParams(dimension_semantics=sem, vmem_limit_bytes=vmem)


def _full(shape):
    n = len(shape)
    return pl.BlockSpec(shape, lambda *_: (0,) * n)


def _low_half(rows):
    return lax.broadcasted_iota(jnp.int32, (rows, LANES), 1) < (LANES // 2)


def _rope_tables(pos, n_rot, head_w):
    half = n_rot // 2
    inv_freq = jnp.power(jnp.float32(ROPE_THETA), -jnp.arange(half, dtype=F32) * (2.0 / n_rot))
    ang = pos.astype(F32)[:, None] * inv_freq[None, :]
    cos, sin = jnp.cos(ang), jnp.sin(ang)
    m = pos.shape[0]
    one = jnp.ones((m, head_w - n_rot), F32)
    zero = jnp.zeros((m, head_w - n_rot), F32)
    zh = jnp.zeros((m, half), F32)
    c = jnp.concatenate([cos, cos, one], axis=1)
    sa = jnp.concatenate([-sin, zh, zero], axis=1)
    sb = jnp.concatenate([zh, sin, zero], axis=1)
    rep = LANES // head_w
    return tuple(jnp.tile(t, (1, rep)) for t in (c, sa, sb))


def _indicator(width, group):
    lane = jnp.arange(width)[:, None] // group
    col = jnp.arange(LANES)[None, :]
    ind = (lane == col).astype(BF)
    return ind, ind.T


def _even_proj_body(x_ref, g_ref, w_ref, nrm_ref, ind_ref, indt_ref, c_ref, sa_ref, sb_ref,
                    a_ref, q_ref, k_ref, v_ref, ki_ref, wi_ref, qih_ref, qil_ref, kb_ref, vb_ref, kih_ref, kil_ref):
    h = _rms(x_ref[0], g_ref[...]).astype(BF)
    z = _dot(h, w_ref[...])
    a_ref[0] = z[:, E_A:E_G] * jax.nn.sigmoid(z[:, E_G:E_Q])
    c, sa, sb = c_ref[...], sa_ref[...], sb_ref[...]
    half = ROT_B // 2
    qk = z[:, E_Q:E_V]
    ssq = _dot2(qk * qk, ind_ref[...])
    r = lax.rsqrt(ssq / DH_B + EPS)
    qk = qk * _dot2(r, indt_ref[...]) * nrm_ref[...]
    for s in range(4):
        slab = _rope128(qk[:, s * LANES:(s + 1) * LANES], c, sa, sb, half)
        q_ref[0, :, s * LANES:(s + 1) * LANES] = (slab * (DH_B ** -0.5)).astype(BF)
    k = _rope128(qk[:, 4 * LANES:5 * LANES], c, sa, sb, half)
    k_ref[0] = k
    kb_ref[0] = k.astype(BF)
    v = z[:, E_V:E_QI]
    v_ref[0] = v
    vb_ref[0] = v.astype(BF)
    for s in range(2):
        lo = E_QI + s * LANES
        hi, lw = _split2(_rope128(z[:, lo:lo + LANES], c, sa, sb, half))
        qih_ref[0, :, s * LANES:(s + 1) * LANES] = hi
        qil_ref[0, :, s * LANES:(s + 1) * LANES] = lw
    ki = _rope128(z[:, E_KI:E_WI], c, sa, sb, half)
    ki_ref[0] = ki[:, :D_IDX]
    kih_ref[0], kil_ref[0] = _split2(ki)
    wi_ref[0] = z[:, E_WI:E_END] * IDX_W_SCALE


def _even_weights(w_in, q_norm, k_norm):
    cuts = [0, 2 * A_CH, 2 * A_CH + 512, 2 * A_CH + 640, 2 * A_CH + 768, 2 * A_CH + 1024, 2 * A_CH + 1088]
    glu = w_in[:, cuts[0]:cuts[1]]
    q = w_in[:, cuts[1]:cuts[2]]
    k = w_in[:, cuts[2]:cuts[3]]
    v = w_in[:, cuts[3]:cuts[4]]
    qi = w_in[:, cuts[4]:cuts[5]]
    ki = w_in[:, cuts[5]:cuts[6]]
    wi = w_in[:, cuts[6]:]
    q = q[:, _DSA_HEAD_PERM]
    pad = jnp.zeros((w_in.shape[0], LANES - H_IDX), w_in.dtype)
    w = jnp.concatenate([glu, q, k, v, qi, ki, ki, wi, pad], axis=1).astype(BF)
    nrm = jnp.concatenate([jnp.tile(q_norm, HB), jnp.tile(k_norm, HKV_B)])[None, :]
    return w, nrm


def _dsa_head_perm():
    j = np.arange(HB * DH_B)
    head = (j // LANES) + (HB // HKV_B) * ((j % LANES) // DH_B)
    return head * DH_B + (j % DH_B)


_DSA_HEAD_PERM = _dsa_head_perm()


def _even_proj(x, g, w, nrm, tables, tb):
    grp, t, _ = x.shape
    ind, indt = _indicator(5 * LANES, DH_B)
    blk = lambda width: pl.BlockSpec((1, tb, width), lambda i, j: (i, j, 0))
    tab = pl.BlockSpec((tb, LANES), lambda i, j: (j, 0))
    sds = lambda width, dt: jax.ShapeDtypeStruct((grp, t, width), dt)
    return pl.pallas_call(
        _even_proj_body,
        grid=(grp, t // tb),
        in_specs=[blk(D_MODEL), _full((1, D_MODEL)), _full(w.shape), _full(nrm.shape),
                  _full(ind.shape), _full(indt.shape), tab, tab, tab],
        out_specs=[blk(A_CH), blk(512), blk(LANES), blk(LANES), blk(D_IDX), blk(LANES),
                   blk(256), blk(256), blk(LANES), blk(LANES), blk(LANES), blk(LANES)],
        out_shape=[sds(A_CH, F32), sds(512, BF), sds(LANES, F32), sds(LANES, F32), sds(D_IDX, F32),
                   sds(LANES, F32), sds(256, BF), sds(256, BF), sds(LANES, BF), sds(LANES, BF),
                   sds(LANES, BF), sds(LANES, BF)],
        compiler_params=_params(("parallel", "parallel")),
        name="even_proj",
    )(x, g[None, :], w, nrm, ind, indt, *tables)


def _conv_tail_body(a_ref, prev_ref, w_ref, b_ref, g_ref, beta_ref, o_ref, buf, shifted, *, tb, pad, stride, rows, carry):
    j = pl.program_id(1)

    @pl.when(j == 0)
    def _():
        buf[0:pad] = prev_ref[0]

    buf[pad:pad + tb] = a_ref[0]
    base = pad - (CONV_A_WIDTH - 1) * stride
    offsets = [base + tap * stride for tap in range(CONV_A_WIDTH)]
    span = pad + tb - SUBLANES
    for r in sorted({o % SUBLANES for o in offsets} - {0}):
        shifted[r - 1, 0:span] = buf[pl.ds(r, span), :]
    for r0 in range(0, tb, rows):
        acc = jnp.broadcast_to(b_ref[...], (rows, A_CH))
        for tap, o in enumerate(offsets):
            r = o % SUBLANES
            src = buf[pl.ds(o + r0, rows), :] if r == 0 else shifted[r - 1, pl.ds(o - r + r0, rows), :]
            acc = acc + w_ref[tap:tap + 1, :] * src
        mu = jnp.mean(acc, axis=-1, keepdims=True)
        d = acc - mu
        var = jnp.mean(d * d, axis=-1, keepdims=True)
        y = d * lax.rsqrt(var + EPS) * g_ref[...] + beta_ref[...]
        o_ref[0, r0:r0 + rows, :] = (y * jax.nn.sigmoid(y)).astype(BF)
    if carry:
        buf[0:pad] = buf[tb:tb + pad]


def _conv_tail(a, prev, w, b, g, beta, tb, stride, rows):
    grp, t, _ = a.shape
    pad = prev.shape[1]
    wp = jnp.concatenate([w, jnp.zeros((1, A_CH), w.dtype)], axis=0)
    n_shift = SUBLANES - 1 if stride % SUBLANES else 1
    body = functools.partial(_conv_tail_body, tb=tb, pad=pad, stride=stride, rows=rows, carry=t > tb)
    return pl.pallas_call(
        body,
        grid=(grp, t // tb),
        in_specs=[pl.BlockSpec((1, tb, A_CH), lambda i, j: (i, j, 0)),
                  pl.BlockSpec((1, pad, A_CH), lambda i, j: (i, 0, 0)),
                  _full(wp.shape), _full((1, A_CH)), _full((1, A_CH)), _full((1, A_CH))],
        out_specs=pl.BlockSpec((1, tb, A_CH), lambda i, j: (i, j, 0)),
        out_shape=jax.ShapeDtypeStruct((grp, t, A_CH), BF),
        scratch_shapes=[pltpu.VMEM((pad + tb, A_CH), F32),
                        pltpu.VMEM((n_shift, pad + tb, A_CH), F32)],
        compiler_params=_params(("parallel", "arbitrary")),
        name="conv_tail",
    )(a, prev, wp, b[None, :], g[None, :], beta[None, :])


def _sort_keys(score):
    score = jnp.where(score == 0.0, 0.0, score)
    bits = pltpu.bitcast(score, jnp.int32)
    return jnp.where(bits < 0, bits ^ jnp.int32(0x7FFFFFFF), bits)


def _kth_largest(key_ref, width, nsel):
    rows = key_ref.shape[0]
    int_min = jnp.int32(-2 ** 31)

    def step(it, thr):
        cand = thr + lax.shift_left(jnp.int32(1), 31 - it)
        cnt = jnp.sum(jnp.where(key_ref[:, 0:width] >= cand, 1.0, 0.0), axis=-1, keepdims=True)
        return jnp.where(cnt >= nsel, cand, thr)

    return lax.fori_loop(0, 32, step, jnp.full((rows, 1), int_min, jnp.int32), unroll=2)


def _select_bias(key_ref, bias_ref, valid, tri_ref, width, nsel, chunk, row_ok=None):
    thr = _kth_largest(key_ref, width, nsel)
    key = key_ref[:, 0:width]
    cnt_gt = jnp.sum(jnp.where(key > thr, 1.0, 0.0), axis=-1, keepdims=True)
    cnt_eq = jnp.sum(jnp.where(key == thr, 1.0, 0.0), axis=-1, keepdims=True)
    need = nsel - cnt_gt
    bias_ref[:, 0:width] = jnp.where(jnp.logical_and(key >= thr, valid), 0.0, NEG)

    surplus = cnt_eq - need
    if row_ok is not None:
        surplus = jnp.where(row_ok, surplus, 0.0)

    @pl.when(jnp.max(surplus) > 0.0)
    def _():
        off = jnp.zeros_like(need)
        for c0 in range(0, width, chunk):
            kc = key_ref[:, c0:c0 + chunk]
            eq = kc == thr
            eqf = jnp.where(eq, 1.0, 0.0)
            rank = _dot(eqf.astype(BF), tri_ref[...]) + off
            take = jnp.logical_or(kc > thr, jnp.logical_and(eq, rank < need))
            bias_ref[:, c0:c0 + chunk] = jnp.where(jnp.logical_and(take, valid[:, c0:c0 + chunk]), 0.0, NEG)
            off = off + jnp.sum(eqf, axis=-1, keepdims=True)


def _strict_upper(n):
    return (jnp.arange(n)[:, None] < jnp.arange(n)[None, :]).astype(BF)


def _stack_heads(x2, low, keep_low):
    zero = jnp.zeros_like(x2)
    return jnp.where(low, x2, zero) if keep_low else jnp.where(low, zero, x2)


DSA_STACK_ROWS = 512


def _dsa_prompt_block(q_ref, qi_ref, wi_ref, k_ref, v_ref, ki_ref, tri_ref, o_ref, key_ref, bias_ref,
                      *, tq, sx, nsel):
    i = pl.program_id(1)
    low = _low_half(tq)
    qpos = i * tq + lax.broadcasted_iota(jnp.int32, (tq, sx), 0)
    kpos = lax.broadcasted_iota(jnp.int32, (tq, sx), 1)
    valid = kpos <= qpos
    per_dot = max(1, DSA_STACK_ROWS // tq)

    def stack(ref, heads, pick_low):
        return jnp.concatenate([_stack_heads(ref[0, :, (h // 2) * LANES:(h // 2 + 1) * LANES], low, pick_low(h))
                                for h in heads], axis=0)

    wi = wi_ref[0]
    score = jnp.zeros((tq, sx), F32)
    for h0 in range(0, H_IDX, per_dot):
        heads = range(h0, min(H_IDX, h0 + per_dot))
        logits = _dot_nt(stack(qi_ref, heads, lambda h: h % 2 == 0), ki_ref[0, 0:sx, :])
        for n, h in enumerate(heads):
            score = score + wi[:, h:h + 1] * jnp.maximum(logits[n * tq:(n + 1) * tq], 0.0)
    score = jnp.where(valid, score, -jnp.inf)
    key_ref[:, 0:sx] = _sort_keys(score)
    _select_bias(key_ref, bias_ref, valid, tri_ref, sx, nsel, LANES)

    bias = bias_ref[:, 0:sx]
    n_pairs = HB // HKV_B
    for p0 in range(0, n_pairs, per_dot):
        pairs = range(p0, min(n_pairs, p0 + per_dot))
        biasn = jnp.concatenate([bias] * len(pairs), axis=0)
        outs = []
        for g in range(HKV_B):
            qs = stack(q_ref, [2 * p for p in pairs], lambda h: g == 0)
            s = _dot_nt(qs, k_ref[0, 0:sx, :]) + biasn
            e = jnp.exp(s - jnp.max(s, axis=-1, keepdims=True))
            l = jnp.sum(e, axis=-1, keepdims=True)
            outs.append(_dot(e.astype(BF), v_ref[0, 0:sx, :]) / l)
        for n, p in enumerate(pairs):
            o_ref[0, :, p * LANES:(p + 1) * LANES] = jnp.where(
                low, outs[0][n * tq:(n + 1) * tq], outs[1][n * tq:(n + 1) * tq]).astype(BF)


def _causal_extents(n_blocks, tq, s):
    nb = 4 if n_blocks % 4 == 0 else 1
    per = n_blocks // nb
    return per, [min(s, (c + 1) * per * tq) for c in range(nb)]


def _dsa_prompt_body(*refs, tq, s, nsel):
    per, extents = _causal_extents(s // tq, tq, s)
    i = pl.program_id(1)
    for c, sx in enumerate(extents):
        pl.when(i // per == c)(functools.partial(_dsa_prompt_block, *refs, tq=tq, sx=sx, nsel=nsel))


def _dsa_prompt(q, qi, wi, kb, vb, ki, tq):
    n, s, _ = q.shape
    nsel = min(TOPK_MAX, s // 4)
    tri = _strict_upper(LANES)
    qblk = lambda width: pl.BlockSpec((1, tq, width), lambda i, j: (i, j, 0))
    sblk = lambda width: pl.BlockSpec((1, s, width), lambda i, j: (i, 0, 0))
    return pl.pallas_call(
        functools.partial(_dsa_prompt_body, tq=tq, s=s, nsel=nsel),
        grid=(n, s // tq),
        in_specs=[qblk(512), qblk(256), qblk(LANES), sblk(LANES), sblk(LANES), sblk(LANES), _full(tri.shape)],
        out_specs=qblk(512),
        out_shape=jax.ShapeDtypeStruct((n, s, 512), BF),
        scratch_shapes=[pltpu.VMEM((tq, s), jnp.int32), pltpu.VMEM((tq, s), F32)],
        compiler_params=_params(("parallel", "parallel")),
        name="dsa_prompt",
    )(q, qi, wi, kb, vb, ki, tri)


def _out_proj_body(x_ref, a_ref, b_ref, wa_ref, wb_ref, o_ref):
    o_ref[0] = x_ref[0] + _dot(a_ref[0], wa_ref[...]) + _dot(b_ref[0], wb_ref[...])


def _out_proj(x, a, b, wa, wb, tb):
    grp, t, _ = x.shape
    blk = lambda width: pl.BlockSpec((1, tb, width), lambda i, j: (i, j, 0))
    return pl.pallas_call(
        _out_proj_body,
        grid=(grp, t // tb),
        in_specs=[blk(D_MODEL), blk(a.shape[-1]), blk(b.shape[-1]), _full(wa.shape), _full(wb.shape)],
        out_specs=blk(D_MODEL),
        out_shape=jax.ShapeDtypeStruct(x.shape, F32),
        compiler_params=_params(("parallel", "parallel")),
        name="out_proj",
    )(x, a, b, wa, wb)


FFN_TILE = 256


def _ffn_body(x_ref, g_ref, wu_ref, cw_ref, cb_ref, wd_ref, prev_ref, o_ref, tail_ref,
              hbuf, carry, ubuf, act, *, tb, pad, stride):
    j = pl.program_id(1)
    hbuf[...] = _rms(x_ref[0], g_ref[...]).astype(BF)

    @pl.when(j == 0)
    def _():
        carry[...] = prev_ref[0]

    for f in range(D_FF // FFN_TILE):
        halves = []
        for part in range(2):
            c0 = part * D_FF + f * FFN_TILE
            u = _dot(hbuf[...], wu_ref[:, c0:c0 + FFN_TILE])
            ubuf[0:pad] = carry[:, c0:c0 + FFN_TILE]
            ubuf[pad:pad + tb] = u
            y = (cw_ref[0:1, c0:c0 + FFN_TILE] * ubuf[pl.ds(pad - 2 * stride, tb), :]
                 + cw_ref[1:2, c0:c0 + FFN_TILE] * ubuf[pl.ds(pad - stride, tb), :]
                 + cw_ref[2:3, c0:c0 + FFN_TILE] * u + cb_ref[:, c0:c0 + FFN_TILE])
            carry[:, c0:c0 + FFN_TILE] = ubuf[tb:tb + pad]
            halves.append(y)
        act[:, f * FFN_TILE:(f + 1) * FFN_TILE] = (halves[1] * jax.nn.sigmoid(halves[1]) * halves[0]).astype(BF)
    o_ref[0] = x_ref[0] + _dot(act[...], wd_ref[...])

    @pl.when(j == pl.num_programs(1) - 1)
    def _():
        tail_ref[0] = carry[...]


def _ffn(x, g, wu, cw, cb, wd, prev, tb, stride):
    grp, t, _ = x.shape
    pad = prev.shape[1]
    cwp = jnp.concatenate([cw, jnp.zeros((SUBLANES - FFN_CONV_WIDTH, 2 * D_FF), cw.dtype)], axis=0)
    blk = pl.BlockSpec((1, tb, D_MODEL), lambda i, j: (i, j, 0))
    pblk = pl.BlockSpec((1, pad, 2 * D_FF), lambda i, j: (i, 0, 0))
    once = lambda shape: pl.BlockSpec(shape, lambda i, j: (0,) * len(shape), pipeline_mode=pl.Buffered(1))
    return pl.pallas_call(
        functools.partial(_ffn_body, tb=tb, pad=pad, stride=stride),
        grid=(grp, t // tb),
        in_specs=[blk, _full((1, D_MODEL)), once(wu.shape), _full(cwp.shape), _full((1, 2 * D_FF)),
                  once(wd.shape), pblk],
        out_specs=[blk, pblk],
        out_shape=[jax.ShapeDtypeStruct(x.shape, F32), jax.ShapeDtypeStruct(prev.shape, F32)],
        scratch_shapes=[pltpu.VMEM((tb, D_MODEL), BF), pltpu.VMEM((pad, 2 * D_FF), F32),
                        pltpu.VMEM((pad + tb, FFN_TILE), F32), pltpu.VMEM((tb, D_FF), BF)],
        compiler_params=_params(("parallel", "arbitrary")),
        name="conv_ffn",
    )(x, g[None, :], wu, cwp, cb[None, :], wd, prev)


def _head_norm128(x, gain, ind_ref, indt_ref, dim):
    ssq = _dot2(x * x, ind_ref[...])
    r = lax.rsqrt(ssq / dim + EPS)
    return x * _dot2(r, indt_ref[...]) * gain


def _odd_proj_body(x_ref, g_ref, w_ref, qan_ref, wqb_ref, kvn_ref, qn_ref, kn_ref, ind_ref, indt_ref,
                   c_ref, sa_ref, sb_ref, glg_ref, glb_ref, wuk_ref, wuv_ref,
                   ckv_ref, kpe_ref, u_ref, v_ref, q_ref, k_ref, vv_ref):
    h = _rms(x_ref[0], g_ref[...]).astype(BF)
    z = _dot(h, w_ref[...])
    c, sa, sb = c_ref[...], sa_ref[...], sb_ref[...]
    half = ROPE_C // 2

    qa = _rms(z[:, O_QA:O_CKV], qan_ref[...]).astype(BF)
    q = _head_norm128(_dot(qa, wqb_ref[...]), qn_ref[...], ind_ref, indt_ref, DQK_C)
    for hd in range(HC):
        sl = slice(hd * LANES, (hd + 1) * LANES)
        q_ref[0, :, sl] = (_rope128(q[:, sl], c, sa, sb, half) * MLA_SCALE).astype(BF)

    ckv = _rms(z[:, O_CKV:O_KPE], kvn_ref[...])
    ckv_ref[0] = ckv
    cb = ckv.astype(BF)
    kpe = z[:, O_KPE:O_U]
    kpe_ref[0] = kpe[:, :ROPE_C]
    kfull = _dot(cb, wuk_ref[...]) + jnp.concatenate([kpe] * HC, axis=1)
    k = _head_norm128(kfull, kn_ref[...], ind_ref, indt_ref, DQK_C)
    for hd in range(HC):
        sl = slice(hd * LANES, (hd + 1) * LANES)
        k_ref[0, :, sl] = _rope128(k[:, sl], c, sa, sb, half).astype(BF)
    vv_ref[0] = _dot(cb, wuv_ref[...]).astype(BF)

    zz = jax.nn.gelu(z[:, O_U:O_END])
    u_ref[0] = zz[:, :D_CH]
    vz = zz[:, D_CH:]
    mu = jnp.mean(vz, axis=-1, keepdims=True)
    d = vz - mu
    var = jnp.mean(d * d, axis=-1, keepdims=True)
    v_ref[0] = d * lax.rsqrt(var + EPS) * glg_ref[...] + glb_ref[...]


def _pad_heads(w, lead):
    z32 = jnp.zeros(lead + (HC, ROPE_C), w.dtype)
    return jnp.concatenate([z32, w, z32], axis=-1).reshape(lead + (HC * LANES,))


def _odd_weights(w_in, w_qb, w_uk, w_uv, q_norm, k_norm):
    d = w_in.shape[0]
    cuts = [Q_LORA, Q_LORA + KV_LORA, Q_LORA + KV_LORA + ROPE_C]
    pad = jnp.zeros((d, LANES - ROPE_C), w_in.dtype)
    w = jnp.concatenate([w_in[:, :cuts[2]], pad, w_in[:, cuts[2]:]], axis=1).astype(BF)
    qb = w_qb.reshape(Q_LORA, HC, DQK_C)
    qb = jnp.concatenate([qb, jnp.zeros((Q_LORA, HC, LANES - DQK_C), w_qb.dtype)], axis=-1)
    wqb = qb.reshape(Q_LORA, HC * LANES).astype(BF)
    wuk_pad = _pad_heads(w_uk, (KV_LORA,)).astype(BF)
    wuk = w_uk.reshape(KV_LORA, HC * NOPE_C).astype(BF)
    wuv = w_uv.reshape(KV_LORA, HC * VD_C).astype(BF)
    gain = lambda g: jnp.tile(jnp.concatenate([g, jnp.zeros((LANES - DQK_C,), g.dtype)]), HC)[None, :]
    return w, wqb, wuk_pad, wuk, wuv, gain(q_norm), gain(k_norm)


def _odd_proj(x, g, w, wqb, wuk_pad, wuv, qan, kvn, qn, kn, glg, glb, tables, tb):
    grp, t, _ = x.shape
    ind, indt = _indicator(HC * LANES, LANES)
    blk = lambda width: pl.BlockSpec((1, tb, width), lambda i, j: (i, j, 0))
    tab = pl.BlockSpec((tb, LANES), lambda i, j: (j, 0))
    sds = lambda width, dt: jax.ShapeDtypeStruct((grp, t, width), dt)
    row = lambda v: v[None, :]
    return pl.pallas_call(
        _odd_proj_body,
        grid=(grp, t // tb),
        in_specs=[blk(D_MODEL), _full((1, D_MODEL)), _full(w.shape), _full((1, Q_LORA)), _full(wqb.shape),
                  _full((1, KV_LORA)), _full(qn.shape), _full(kn.shape), _full(ind.shape), _full(indt.shape),
                  tab, tab, tab, _full((1, D_CH)), _full((1, D_CH)), _full(wuk_pad.shape), _full(wuv.shape)],
        out_specs=[blk(KV_LORA), blk(ROPE_C), blk(D_CH), blk(D_CH), blk(HC * LANES), blk(HC * LANES),
                   blk(HC * VD_C)],
        out_shape=[sds(KV_LORA, F32), sds(ROPE_C, F32), sds(D_CH, F32), sds(D_CH, F32),
                   sds(HC * LANES, BF), sds(HC * LANES, BF), sds(HC * VD_C, BF)],
        compiler_params=_params(("parallel", "parallel")),
        name="odd_proj",
    )(x, row(g), w, row(qan), wqb, row(kvn), qn, kn, ind, indt, *tables, row(glg), row(glb), wuk_pad, wuv)


def _gmlp_prompt_body(u_ref, v_ref, w_ref, b_ref, o_ref, *, tb):
    rows = D_GROUPS * CHUNK
    t_in = lax.broadcasted_iota(jnp.int32, (rows, CHUNK), 0) % CHUNK
    s_in = lax.broadcasted_iota(jnp.int32, (rows, CHUNK), 1)
    w = jnp.where(s_in <= t_in, w_ref[...], 0.0).astype(BF)
    grp = lax.broadcasted_iota(jnp.int32, (CHUNK, D_CH), 1) // (D_CH // D_GROUPS)
    for c0 in range(0, tb, CHUNK):
        y = _dot(w, v_ref[0, c0:c0 + CHUNK, :].astype(BF))
        mix = b_ref[...]
        for gi in range(D_GROUPS):
            mix = mix + jnp.where(grp == gi, y[gi * CHUNK:(gi + 1) * CHUNK], 0.0)
        o_ref[0, c0:c0 + CHUNK, :] = (u_ref[0, c0:c0 + CHUNK, :] * mix).astype(BF)


def _gmlp_prompt(u, v, w_s, b_s, tb):
    n, s, _ = u.shape
    w = w_s.reshape(D_GROUPS * CHUNK, CHUNK)
    b = jnp.repeat(b_s.T, D_CH // D_GROUPS, axis=1)
    blk = pl.BlockSpec((1, tb, D_CH), lambda i, j: (i, j, 0))
    return pl.pallas_call(
        functools.partial(_gmlp_prompt_body, tb=tb),
        grid=(n, s // tb),
        in_specs=[blk, blk, _full(w.shape), _full(b.shape)],
        out_specs=blk,
        out_shape=jax.ShapeDtypeStruct(u.shape, BF),
        compiler_params=_params(("parallel", "parallel")),
        name="gmlp_prompt",
    )(u, v, w, b)


def _gmlp_sample_body(u_ref, v_ref, w_ref, b_ref, o_ref, *, nb, nt):
    for t in range(nt):
        mix = jnp.broadcast_to(b_ref[t:t + 1, :], (nb, D_CH))
        for s in range(t + 1):
            mix = mix + w_ref[t * nt + s:t * nt + s + 1, :] * v_ref[s * nb:(s + 1) * nb, :]
        o_ref[t * nb:(t + 1) * nb, :] = (u_ref[t * nb:(t + 1) * nb, :] * mix).astype(BF)


def _gmlp_sample(u, v, w_s, b_s, nb, nt):
    lane = D_CH // D_GROUPS
    w = jnp.repeat(jnp.transpose(w_s[:, :nt, :nt], (1, 2, 0)).reshape(nt * nt, D_GROUPS), lane, axis=1)
    b = jnp.repeat(b_s.T[:nt], lane, axis=1)
    return pl.pallas_call(
        functools.partial(_gmlp_sample_body, nb=nb, nt=nt),
        out_shape=jax.ShapeDtypeStruct(u.shape, BF),
        name="gmlp_sample",
    )(u, v, w, b)


def _mla_prompt_block(q_ref, k_ref, v_ref, o_ref, *, tq, sx):
    i = pl.program_id(1)
    low = _low_half(tq)
    qpos = i * tq + lax.broadcasted_iota(jnp.int32, (tq, sx), 0)
    kpos = lax.broadcasted_iota(jnp.int32, (tq, sx), 1)
    bias = jnp.where(kpos <= qpos, 0.0, NEG)
    for p in range(HC // 2):
        outs = []
        for hd in (2 * p, 2 * p + 1):
            sl = slice(hd * LANES, (hd + 1) * LANES)
            s = _dot_nt(q_ref[0, :, sl], k_ref[0, 0:sx, sl]) + bias
            e = jnp.exp(s - jnp.max(s, axis=-1, keepdims=True))
            l = jnp.sum(e, axis=-1, keepdims=True)
            outs.append(_dot(e.astype(BF), v_ref[0, 0:sx, p * LANES:(p + 1) * LANES]) / l)
        o_ref[0, :, p * LANES:(p + 1) * LANES] = jnp.where(low, outs[0], outs[1]).astype(BF)


def _mla_prompt_body(*refs, tq, s):
    per, extents = _causal_extents(s // tq, tq, s)
    i = pl.program_id(1)
    for c, sx in enumerate(extents):
        pl.when(i // per == c)(functools.partial(_mla_prompt_block, *refs, tq=tq, sx=sx))


def _mla_prompt(q, k, v, tq):
    n, s, _ = q.shape
    return pl.pallas_call(
        functools.partial(_mla_prompt_body, tq=tq, s=s),
        grid=(n, s // tq),
        in_specs=[pl.BlockSpec((1, tq, HC * LANES), lambda i, j: (i, j, 0)),
                  pl.BlockSpec((1, s, HC * LANES), lambda i, j: (i, 0, 0)),
                  pl.BlockSpec((1, s, HC * VD_C), lambda i, j: (i, 0, 0))],
        out_specs=pl.BlockSpec((1, tq, HC * VD_C), lambda i, j: (i, j, 0)),
        out_shape=jax.ShapeDtypeStruct((n, s, HC * VD_C), BF),
        compiler_params=_params(("parallel", "parallel")),
        name="mla_prompt",
    )(q, k, v)


PAGES_PER_STEP = 32


def _page_specs(cache_shape, layer):
    _, _, rows, width = cache_shape

    def spec(slot):
        return pl.BlockSpec((None, None, rows, width),
                            lambda s, j, pt: (layer, pt[s, j * PAGES_PER_STEP + slot], 0, 0))

    return [spec(slot) for slot in range(PAGES_PER_STEP)]


def _cat_pages(refs, axis):
    return jnp.concatenate([r[...] for r in refs], axis=axis)


def _idx_scores(qh, ql, wi, kh, kl):
    both = _dot(jnp.concatenate([qh, ql], axis=0), kh)
    rows = qh.shape[0]
    logits = both[0:rows] + both[rows:2 * rows] + _dot(qh, kl)
    score = jnp.zeros((SUBLANES, logits.shape[1]), F32)
    for h in range(H_IDX):
        sl = slice(h * SUBLANES, (h + 1) * SUBLANES)
        score = score + wi[sl, 0:1] * jnp.maximum(logits[sl], 0.0)
    return score


def _dsa_sample_score_body(pt_ref, qh_ref, ql_ref, wi_ref, *rest):
    del pt_ref
    pages, o_ref = rest[:PAGES_PER_STEP], rest[PAGES_PER_STEP]
    kh, kl = _split2(_cat_pages(pages, 1))
    o_ref[0] = _idx_scores(qh_ref[0], ql_ref[0], wi_ref[0], kh, kl)


def _dsa_sample_scores(page_table, qh, ql, wi, cache_ki, layer):
    n, n_pages = page_table.shape
    steps = n_pages // PAGES_PER_STEP
    span = PAGES_PER_STEP * PAGE_SIZE
    per_seq = lambda shape: pl.BlockSpec((1,) + shape, lambda s, j, pt: (s, 0, 0))
    grid_spec = pltpu.PrefetchScalarGridSpec(
        num_scalar_prefetch=1,
        grid=(n, steps),
        in_specs=[per_seq((4 * SUBLANES, D_IDX)), per_seq((4 * SUBLANES, D_IDX)), per_seq((4 * SUBLANES, LANES))]
        + _page_specs(cache_ki.shape, layer),
        out_specs=pl.BlockSpec((1, SUBLANES, span), lambda s, j, pt: (s, 0, j)),
    )
    return pl.pallas_call(
        _dsa_sample_score_body,
        grid_spec=grid_spec,
        out_shape=jax.ShapeDtypeStruct((n, SUBLANES, n_pages * PAGE_SIZE), F32),
        compiler_params=_params(("parallel", "arbitrary")),
        name="dsa_sample_scores",
    )(page_table, qh, ql, wi, *([cache_ki] * PAGES_PER_STEP))


SELECT_SEQS = 8


def _dsa_sample_select_body(score_ref, qh_ref, ql_ref, wi_ref, knh_ref, knl_ref, tri_ref, bias_ref, key_ref,
                            *, past, nsel, nt):
    width = past + PAGE_SIZE
    rows = SELECT_SEQS * SUBLANES
    slot = lax.broadcasted_iota(jnp.int32, (rows, width), 0) % SUBLANES
    kpos = lax.broadcasted_iota(jnp.int32, (rows, width), 1)
    valid = kpos <= past + slot
    for g in range(SELECT_SEQS):
        sl = slice(g * SUBLANES, (g + 1) * SUBLANES)
        new = _idx_scores(qh_ref[g], ql_ref[g], wi_ref[g], knh_ref[g], knl_ref[g])
        key_ref[sl, 0:past] = _sort_keys(score_ref[g])
        key_ref[sl, past:width] = _sort_keys(jnp.where(valid[sl, past:width], new, -jnp.inf))
    row_ok = lax.broadcasted_iota(jnp.int32, (rows, 1), 0) % SUBLANES < nt
    _select_bias(key_ref, bias_ref, valid, tri_ref, width, nsel, LANES, row_ok)


def _dsa_sample_select(score, qh, ql, wi, knh, knl, nt):
    n, _, past = score.shape
    nsel = min(TOPK_MAX, (past + nt) // 4)
    tri = _strict_upper(LANES)
    width = past + PAGE_SIZE
    rows = SELECT_SEQS * SUBLANES
    blk = lambda a: pl.BlockSpec((SELECT_SEQS,) + a.shape[1:], lambda i: (i, 0, 0))
    return pl.pallas_call(
        functools.partial(_dsa_sample_select_body, past=past, nsel=nsel, nt=nt),
        grid=(n // SELECT_SEQS,),
        in_specs=[blk(score), blk(qh), blk(ql), blk(wi), blk(knh), blk(knl), _full(tri.shape)],
        out_specs=pl.BlockSpec((rows, width), lambda i: (i, 0)),
        out_shape=jax.ShapeDtypeStruct((n * SUBLANES, width), F32),
        scratch_shapes=[pltpu.VMEM((rows, width), jnp.int32)],
        compiler_params=_params(("parallel",)),
        name="dsa_sample_select",
    )(score, qh, ql, wi, knh, knl, tri)


def _dsa_sample_attend_body(pt_ref, bias_ref, btail_ref, q_ref, kn_ref, vn_ref, *rest):
    del pt_ref
    kpages = rest[:PAGES_PER_STEP]
    vpages = rest[PAGES_PER_STEP:2 * PAGES_PER_STEP]
    o_ref, q64_ref, m_ref, l_ref, acc_ref = rest[2 * PAGES_PER_STEP:]
    j = pl.program_id(1)
    rows8 = SUBLANES
    per = HB // HKV_B

    @pl.when(j == 0)
    def _():
        low = _low_half(rows8)
        q8 = q_ref[0].astype(F32)
        q64_ref[...] = jnp.concatenate(
            [_stack_heads(q8[:, (h % per) * LANES:(h % per + 1) * LANES], low, h < per) for h in range(HB)],
            axis=0).astype(BF)
        m_ref[...] = jnp.full(m_ref.shape, NEG, F32)
        l_ref[...] = jnp.zeros(l_ref.shape, F32)
        acc_ref[...] = jnp.zeros(acc_ref.shape, F32)

    def fold(bias8, k_t, v_t):
        s = _dot(q64_ref[...], k_t) + jnp.concatenate([bias8] * HB, axis=0)
        m_old = m_ref[...]
        m_new = jnp.maximum(m_old, jnp.max(s, axis=-1, keepdims=True))
        alpha = jnp.exp(m_old - m_new)
        e = jnp.exp(s - m_new)
        l_ref[...] = alpha * l_ref[...] + jnp.sum(e, axis=-1, keepdims=True)
        acc_ref[...] = alpha * acc_ref[...] + _dot_nt(e.astype(BF), v_t)
        m_ref[...] = m_new

    fold(bias_ref[0], _cat_pages(kpages, 1).astype(BF), _cat_pages(vpages, 1).astype(BF))

    @pl.when(j == pl.num_programs(1) - 1)
    def _():
        fold(btail_ref[0], kn_ref[0], vn_ref[0])
        out = acc_ref[...] / l_ref[...]
        low = _low_half(rows8)
        for p in range(per):
            o_ref[0, :, p * LANES:(p + 1) * LANES] = jnp.where(
                low, out[p * rows8:(p + 1) * rows8], out[(per + p) * rows8:(per + p + 1) * rows8]).astype(BF)


def _dsa_sample_attend(page_table, bias, q8, kn, vn, cache_k, cache_v, layer):
    n, n_pages = page_table.shape
    span = PAGES_PER_STEP * PAGE_SIZE
    per_seq = lambda shape: pl.BlockSpec((1,) + shape, lambda s, j, pt: (s, 0, 0))
    grid_spec = pltpu.PrefetchScalarGridSpec(
        num_scalar_prefetch=1,
        grid=(n, n_pages // PAGES_PER_STEP),
        in_specs=[pl.BlockSpec((1, SUBLANES, span), lambda s, j, pt: (s, 0, j)),
                  pl.BlockSpec((1, SUBLANES, PAGE_SIZE), lambda s, j, pt: (s, 0, n_pages)),
                  per_seq((SUBLANES, HB * DH_B)), per_seq((LANES, PAGE_SIZE)), per_seq((LANES, PAGE_SIZE))]
        + _page_specs(cache_k.shape, layer) + _page_specs(cache_v.shape, layer),
        out_specs=per_seq((SUBLANES, HB * DH_B)),
        scratch_shapes=[pltpu.VMEM((HB * SUBLANES, LANES), BF), pltpu.VMEM((HB * SUBLANES, 1), F32),
                        pltpu.VMEM((HB * SUBLANES, 1), F32), pltpu.VMEM((HB * SUBLANES, LANES), F32)],
    )
    return pl.pallas_call(
        _dsa_sample_attend_body,
        grid_spec=grid_spec,
        out_shape=jax.ShapeDtypeStruct((n, SUBLANES, HB * DH_B), BF),
        compiler_params=_params(("parallel", "arbitrary")),
        name="dsa_sample_attend",
    )(page_table, bias, bias, q8, kn, vn, *([cache_k] * PAGES_PER_STEP), *([cache_v] * PAGES_PER_STEP))


MLA_ROWS = HC * SUBLANES


def _mla_sample_body(pt_ref, q_ref, kn_ref, vn_ref, wukp_ref, wukt_ref, wuv_ref, gk_ref, ct_ref, st_ref,
                     *rest, nt):
    del pt_ref
    cpages = rest[:PAGES_PER_STEP]
    ppages = rest[PAGES_PER_STEP:2 * PAGES_PER_STEP]
    o_ref, qbd_ref, lhs_ref, qrot_ref, m_ref, l_ref, acc_ref = rest[2 * PAGES_PER_STEP:]
    j = pl.program_id(1)
    rows = MLA_ROWS
    feat = HC * NOPE_C

    @pl.when(j == 0)
    def _():
        q8 = q_ref[0].astype(F32)
        rhead = lax.broadcasted_iota(jnp.int32, (rows, HC * LANES), 0) // SUBLANES
        lhead = lax.broadcasted_iota(jnp.int32, (rows, HC * LANES), 1) // LANES
        qbd = jnp.where(rhead == lhead, jnp.concatenate([q8] * HC, axis=0), 0.0)
        qbd_ref[...] = qbd.astype(BF)
        lhs_ref[0:feat, :] = wukt_ref[...]
        lhs_ref[feat:feat + rows, :] = _dot_nt((qbd * gk_ref[...]).astype(BF), wukp_ref[...]).astype(BF)
        folded = qbd[:, 0:LANES]
        for hd in range(1, HC):
            folded = folded + qbd[:, hd * LANES:(hd + 1) * LANES]
        lane = lax.broadcasted_iota(jnp.int32, (rows, LANES), 1)
        half = ROPE_C // 2
        swapped = jnp.where(lane < half, pltpu.roll(folded, LANES - half, 1),
                            jnp.where(lane < ROPE_C, -pltpu.roll(folded, half, 1), 0.0))
        gpe = gk_ref[:, 0:LANES]
        qrot_ref[...] = jnp.concatenate([(folded * gpe)[:, 0:ROPE_C], (swapped * gpe)[:, 0:ROPE_C]],
                                        axis=1).astype(BF)
        m_ref[...] = jnp.full(m_ref.shape, NEG, F32)
        l_ref[...] = jnp.zeros(l_ref.shape, F32)
        acc_ref[...] = jnp.zeros(acc_ref.shape, F32)

    cb = _cat_pages(cpages, 0).astype(BF)
    kpe = _cat_pages(ppages, 1)
    big = _dot_nt(lhs_ref[...], cb)
    kp2 = jnp.sum(kpe * kpe, axis=0, keepdims=True)
    rot = _dot(qrot_ref[...], jnp.concatenate([kpe * ct_ref[...], kpe * st_ref[...]], axis=0).astype(BF))
    parts = []
    for hd in range(HC):
        kn = big[hd * NOPE_C:(hd + 1) * NOPE_C]
        r = lax.rsqrt((jnp.sum(kn * kn, axis=0, keepdims=True) + kp2) / DQK_C + EPS)
        sl = slice(hd * SUBLANES, (hd + 1) * SUBLANES)
        parts.append(r * (big[feat + hd * SUBLANES:feat + (hd + 1) * SUBLANES] + rot[sl]))
    s = jnp.concatenate(parts, axis=0)

    m_old = m_ref[...]
    m_new = jnp.maximum(m_old, jnp.max(s, axis=-1, keepdims=True))
    alpha = jnp.exp(m_old - m_new)
    e = jnp.exp(s - m_new)
    l_ref[...] = alpha * l_ref[...] + jnp.sum(e, axis=-1, keepdims=True)
    acc_ref[...] = alpha * acc_ref[...] + _dot(e.astype(BF), cb)
    m_ref[...] = m_new

    @pl.when(j == pl.num_programs(1) - 1)
    def _():
        s_new = _dot_nt(qbd_ref[...], kn_ref[0])
        t_row = lax.broadcasted_iota(jnp.int32, s_new.shape, 0) % SUBLANES
        col = lax.broadcasted_iota(jnp.int32, s_new.shape, 1)
        s_new = jnp.where(jnp.logical_and(col <= t_row, col < nt), s_new, NEG)
        m_old = m_ref[...]
        m_fin = jnp.maximum(m_old, jnp.max(s_new, axis=-1, keepdims=True))
        alpha = jnp.exp(m_old - m_fin)
        e = jnp.exp(s_new - m_fin)
        l = alpha * l_ref[...] + jnp.sum(e, axis=-1, keepdims=True)
        out = (_dot((alpha * acc_ref[...]).astype(BF), wuv_ref[...]) + _dot(e.astype(BF), vn_ref[0])) / l
        lhead = lax.broadcasted_iota(jnp.int32, (SUBLANES, HC * VD_C), 1) // VD_C
        res = jnp.zeros((SUBLANES, HC * VD_C), F32)
        for hd in range(HC):
            res = res + jnp.where(lhead == hd, out[hd * SUBLANES:(hd + 1) * SUBLANES], 0.0)
        o_ref[0] = res.astype(BF)


def _mla_sample(page_table, q8, kn, vn, wuk_pad, wuk, wuv, gk, ctab, stab, cache_ckv, cache_kpe, layer, nt):
    n, n_pages = page_table.shape
    span = PAGES_PER_STEP * PAGE_SIZE
    rows = MLA_ROWS
    wukt = wuk.T
    per_seq = lambda shape: pl.BlockSpec((1,) + shape, lambda s, j, pt: (s, 0, 0))
    const = lambda a: pl.BlockSpec(a.shape, lambda s, j, pt: (0,) * a.ndim)
    tab = pl.BlockSpec((ROPE_C, span), lambda s, j, pt: (0, j))
    grid_spec = pltpu.PrefetchScalarGridSpec(
        num_scalar_prefetch=1,
        grid=(n, n_pages // PAGES_PER_STEP),
        in_specs=[per_seq((SUBLANES, HC * LANES)), per_seq((PAGE_SIZE, HC * LANES)), per_seq((PAGE_SIZE, HC * VD_C)),
                  const(wuk_pad), const(wukt), const(wuv), const(gk), tab, tab]
        + _page_specs(cache_ckv.shape, layer) + _page_specs(cache_kpe.shape, layer),
        out_specs=per_seq((SUBLANES, HC * VD_C)),
        scratch_shapes=[pltpu.VMEM((rows, HC * LANES), BF), pltpu.VMEM((HC * NOPE_C + rows, KV_LORA), BF),
                        pltpu.VMEM((rows, 2 * ROPE_C), BF),
                        pltpu.VMEM((rows, 1), F32), pltpu.VMEM((rows, 1), F32), pltpu.VMEM((rows, KV_LORA), F32)],
    )
    return pl.pallas_call(
        functools.partial(_mla_sample_body, nt=nt),
        grid_spec=grid_spec,
        out_shape=jax.ShapeDtypeStruct((n, SUBLANES, HC * VD_C), BF),
        compiler_params=_params(("parallel", "arbitrary")),
        name="mla_sample",
    )(page_table, q8, kn, vn, wuk_pad, wukt, wuv, gk, ctab, stab,
      *([cache_ckv] * PAGES_PER_STEP), *([cache_kpe] * PAGES_PER_STEP))


PROMPT_ROWS = 512
CONV_ROWS = 256
DSA_QBLOCK = 256
MLA_QBLOCK = 512


def kernel(x_prompt, x_sample, cache_dsa_k, cache_dsa_v, cache_dsa_kidx, state_conv_a, cache_mla_ckv, cache_mla_kpe, state_ffn_conv, page_table, norm_mix, norm_ffn, w_in_e, conv_a_w, conv_a_b, conv_a_ln_g, conv_a_ln_b, q_norm_b, k_norm_b, w_out_e, w_in_o, q_a_norm, w_qb, kv_a_norm, w_uk, w_uv, q_norm_c, k_norm_c, gmlp_ln_g, gmlp_ln_b, w_spatial, b_spatial, w_out_o, w_up, ffn_conv_w, ffn_conv_b, w_down):
    bsz, seq, _ = x_prompt.shape
    nb, nt, _ = x_sample.shape
    depth = norm_mix.shape[0]
    past = page_table.shape[1] * PAGE_SIZE
    srows = nt * nb
    pos_p = jnp.arange(seq)
    pos_s = past + jnp.repeat(jnp.arange(nt), nb)

    def to_rows(a):
        return jnp.transpose(a, (1, 0, 2)).reshape(1, a.shape[1] * nb, a.shape[2])

    def to_seq(a):
        return jnp.transpose(a.reshape(-1, nb, a.shape[-1]), (1, 0, 2))

    def pad_rows(a, rows):
        return jnp.pad(a, ((0, 0), (0, rows - a.shape[1]), (0, 0)))

    def idx_stack(a):
        w = a.shape[-1] // H_IDX
        a = jnp.transpose(to_seq(a).reshape(nb, nt, H_IDX, w), (0, 2, 1, 3))
        a = jnp.pad(a, ((0, 0), (0, 0), (0, SUBLANES - nt), (0, 0)))
        return a.reshape(nb, H_IDX * SUBLANES, w)

    yp = x_prompt
    ys = to_rows(x_sample)
    pk, pv, pki, pca, pckv, pkpe, pff = [], [], [], [], [], [], []
    sk, sv, ski, sca, sckv, skpe, sgv, sff = [], [], [], [], [], [], [], []

    for layer in range(depth):
        if layer % 2 == 0:
            e = layer // 2
            w, nrm = _even_weights(w_in_e[e], q_norm_b[e], k_norm_b[e])
            tabs_p = _rope_tables(pos_p, ROT_B, DH_B)
            tabs_s = _rope_tables(pos_s, ROT_B, DH_B)
            ap, qp, kp, vp, kip, wip, qihp, qilp, kbp, vbp, kihp, kilp = _even_proj(
                yp, norm_mix[layer], w, nrm, tabs_p, PROMPT_ROWS)
            a_s, qs, ks, vs, kis, wis, qihs, qils, kbs, vbs, kihs, kils = _even_proj(
                ys, norm_mix[layer], w, nrm, tabs_s, srows)

            hist = CONV_A_WIDTH - 1
            conv = (conv_a_w[e], conv_a_b[e], conv_a_ln_g[e], conv_a_ln_b[e])
            cp = _conv_tail(ap, jnp.zeros((bsz, 32, A_CH), F32), *conv, CONV_ROWS, 1, 32)
            cs = _conv_tail(a_s, to_rows(state_conv_a[e]), *conv, srows, nb, nb)
            pca.append(ap[:, seq - hist:, :])
            sca.append(jnp.concatenate([state_conv_a[e], to_seq(a_s)], axis=1)[:, nt:, :])

            bp = _dsa_prompt(qp, qihp, wip, kbp, vbp, kihp, DSA_QBLOCK)

            qh32, ql32 = idx_stack(qihs), idx_stack(qils)
            wi32 = jnp.broadcast_to(idx_stack(wis[..., :H_IDX]), (nb, H_IDX * SUBLANES, LANES))
            n_pool = cache_dsa_k.shape[1]
            kv_t = lambda c: jnp.transpose(c, (0, 1, 3, 4, 2)).reshape(-1, n_pool, HKV_B * DH_B, PAGE_SIZE)
            new_t = lambda a: jnp.transpose(pad_rows(to_seq(a), PAGE_SIZE), (0, 2, 1))
            scores = _dsa_sample_scores(page_table, qh32, ql32, wi32, jnp.transpose(cache_dsa_kidx, (0, 1, 3, 2)), e)
            bias = _dsa_sample_select(scores, qh32, ql32, wi32, new_t(kihs[..., :D_IDX]), new_t(kils[..., :D_IDX]), nt)
            bs8 = _dsa_sample_attend(page_table, bias.reshape(nb, SUBLANES, -1), pad_rows(to_seq(qs), SUBLANES),
                                     new_t(kbs), new_t(vbs), kv_t(cache_dsa_k), kv_t(cache_dsa_v), e)
            bs = to_rows(bs8[:, :nt])

            wa = w_out_e[e][:A_CH].astype(BF)
            wb = w_out_e[e][A_CH:][_DSA_HEAD_PERM].astype(BF)
            yp = _out_proj(yp, cp, bp, wa, wb, PROMPT_ROWS)
            ys = _out_proj(ys, cs, bs, wa, wb, srows)

            pk.append(kp.reshape(bsz, seq, HKV_B, DH_B))
            pv.append(vp.reshape(bsz, seq, HKV_B, DH_B))
            pki.append(kip)
            sk.append(to_seq(ks).reshape(nb, nt, HKV_B, DH_B))
            sv.append(to_seq(vs).reshape(nb, nt, HKV_B, DH_B))
            ski.append(to_seq(kis))
        else:
            o = layer // 2
            w, wqb, wukp, wuk, wuv, qn, kn = _odd_weights(w_in_o[o], w_qb[o], w_uk[o], w_uv[o],
                                                          q_norm_c[o], k_norm_c[o])
            tabs_p = _rope_tables(pos_p, ROPE_C, LANES)
            tabs_s = _rope_tables(pos_s, ROPE_C, LANES)
            rest = (w, wqb, wukp, wuv, q_a_norm[o], kv_a_norm[o], qn, kn, gmlp_ln_g[o], gmlp_ln_b[o])
            ckvp, kpep, up, vp, q_p, k_p, v_p = _odd_proj(yp, norm_mix[layer], *rest, tabs_p, PROMPT_ROWS)
            ckvs, kpes, us, vs, q_s, k_s, v_s = _odd_proj(ys, norm_mix[layer], *rest, tabs_s, srows)

            mp = _mla_prompt(q_p, k_p, v_p, MLA_QBLOCK)
            gp = _gmlp_prompt(up, vp, w_spatial[o], b_spatial[o], PROMPT_ROWS)
            gs = _gmlp_sample(us[0], vs[0], w_spatial[o], b_spatial[o], nb, nt)[None]

            half = ROPE_C // 2
            inv_freq = jnp.power(jnp.float32(ROPE_THETA), -jnp.arange(half, dtype=F32) * (2.0 / ROPE_C))
            ang = jnp.arange(past).astype(F32)[:, None] * inv_freq[None, :]
            ctab = jnp.tile(jnp.cos(ang), (1, 2)).T
            stab = jnp.tile(jnp.sin(ang), (1, 2)).T
            ms8 = _mla_sample(page_table, pad_rows(to_seq(q_s), SUBLANES), pad_rows(to_seq(k_s), PAGE_SIZE),
                              pad_rows(to_seq(v_s), PAGE_SIZE), wukp, wuk, wuv, kn, ctab, stab,
                              cache_mla_ckv, jnp.transpose(cache_mla_kpe, (0, 1, 3, 2)), o, nt)
            ms = to_rows(ms8[:, :nt])

            wa = w_out_o[o][:HC * VD_C].astype(BF)
            wb = w_out_o[o][HC * VD_C:].astype(BF)
            yp = _out_proj(yp, mp, gp, wa, wb, PROMPT_ROWS)
            ys = _out_proj(ys, ms, gs, wa, wb, srows)

            pckv.append(ckvp)
            pkpe.append(kpep)
            sckv.append(to_seq(ckvs))
            skpe.append(to_seq(kpes))
            sgv.append(to_seq(vs))

        keep = FFN_CONV_WIDTH - 1
        ffn = (norm_ffn[layer], w_up[layer].astype(BF), ffn_conv_w[layer], ffn_conv_b[layer],
               w_down[layer].astype(BF))
        yp, tail_p = _ffn(yp, *ffn, jnp.zeros((bsz, SUBLANES, 2 * D_FF), F32), PROMPT_ROWS, 1)
        ys, tail_s = _ffn(ys, *ffn, to_rows(state_ffn_conv[layer]), srows, nb)
        pff.append(tail_p[:, SUBLANES - keep:, :])
        sff.append(to_seq(tail_s))

    return (yp, to_seq(ys),
            jnp.stack(pk), jnp.stack(pv), jnp.stack(pki), jnp.stack(pca),
            jnp.stack(pckv), jnp.stack(pkpe), jnp.stack(pff),
            jnp.stack(sk), jnp.stack(sv), jnp.stack(ski), jnp.stack(sca),
            jnp.stack(sckv), jnp.stack(skpe), jnp.stack(sgv), jnp.stack(sff))
```

```python
import functools

import numpy as np
import jax
import jax.numpy as jnp
from jax import lax
from jax.experimental import pallas as pl
from jax.experimental.pallas import tpu as pltpu

F32 = jnp.float32
BF = jnp.bfloat16

D_MODEL = 1024
PAGE_SIZE = 128
ROPE_THETA = 500000.0
EPS = 1e-6

A_CH = D_MODEL // 2
CONV_A_WIDTH = 31

DH_B = 64
HB = (D_MODEL // 2) // DH_B
HKV_B = 2
ROT_B = DH_B // 4
H_IDX = 4
D_IDX = 64
ROT_IDX = D_IDX // 4
TOPK_MAX = 256
IDX_W_SCALE = (H_IDX * D_IDX) ** -0.5

VD_C = 64
HC = (D_MODEL // 2) // VD_C
NOPE_C = 64
ROPE_C = 32
DQK_C = NOPE_C + ROPE_C
Q_LORA = 3 * D_MODEL // 8
KV_LORA = D_MODEL // 4
MLA_SCALE = DQK_C ** -0.5

D_CH = D_MODEL // 2
D_GROUPS = 8
CHUNK = 128

D_FF = 11 * D_MODEL // 4
FFN_CONV_WIDTH = 3

LANES = 128
SUBLANES = 8
NEG = -1e30
VMEM_LIMIT = 56 * 1024 * 1024

E_A, E_G, E_Q, E_K, E_V, E_QI, E_KI, E_WI, E_END = 0, 512, 1024, 1536, 1664, 1792, 2048, 2176, 2304
O_QA, O_CKV, O_KPE, O_U, O_V, O_END = 0, 384, 640, 768, 1280, 1792


def _dot(a, b):
    return jnp.dot(a, b, preferred_element_type=F32)


def _dot_nt(a, b):
    return lax.dot_general(a, b, (((1,), (1,)), ((), ())), preferred_element_type=F32)


def _split2(x):
    hi = x.astype(BF)
    lo = (x - hi.astype(F32)).astype(BF)
    return hi, lo


def _dot2(x, m):
    hi, lo = _split2(x)
    return _dot(hi, m) + _dot(lo, m)


def _rms(x, g):
    return x * lax.rsqrt(jnp.mean(x * x, axis=-1, keepdims=True) + EPS) * g


def _rope128(x, c, sa, sb, half):
    return x * c + pltpu.roll(x, LANES - half, 1) * sa + pltpu.roll(x, half, 1) * sb


def _params(sem, vmem=VMEM_LIMIT):
    return pltpu.CompilerParams(dimension_semantics=sem, vmem_limit_bytes=vmem)


def _full(shape):
    n = len(shape)
    return pl.BlockSpec(shape, lambda *_: (0,) * n)


def _low_half(rows):
    return lax.broadcasted_iota(jnp.int32, (rows, LANES), 1) < (LANES // 2)


def _rope_tables(pos, n_rot, head_w):
    half = n_rot // 2
    inv_freq = jnp.power(jnp.float32(ROPE_THETA), -jnp.arange(half, dtype=F32) * (2.0 / n_rot))
    ang = pos.astype(F32)[:, None] * inv_freq[None, :]
    cos, sin = jnp.cos(ang), jnp.sin(ang)
    m = pos.shape[0]
    one = jnp.ones((m, head_w - n_rot), F32)
    zero = jnp.zeros((m, head_w - n_rot), F32)
    zh = jnp.zeros((m, half), F32)
    c = jnp.concatenate([cos, cos, one], axis=1)
    sa = jnp.concatenate([-sin, zh, zero], axis=1)
    sb = jnp.concatenate([zh, sin, zero], axis=1)
    rep = LANES // head_w
    return tuple(jnp.tile(t, (1, rep)) for t in (c, sa, sb))


def _indicator(width, group):
    lane = jnp.arange(width)[:, None] // group
    col = jnp.arange(LANES)[None, :]
    ind = (lane == col).astype(BF)
    return ind, ind.T


def _even_proj_body(x_ref, g_ref, w_ref, nrm_ref, ind_ref, indt_ref, c_ref, sa_ref, sb_ref,
                    a_ref, q_ref, k_ref, v_ref, ki_ref, wi_ref, qih_ref, qil_ref, kb_ref, vb_ref, kih_ref, kil_ref):
    h = _rms(x_ref[0], g_ref[...]).astype(BF)
    z = _dot(h, w_ref[...])
    a_ref[0] = z[:, E_A:E_G] * jax.nn.sigmoid(z[:, E_G:E_Q])
    c, sa, sb = c_ref[...], sa_ref[...], sb_ref[...]
    half = ROT_B // 2
    qk = z[:, E_Q:E_V]
    ssq = _dot2(qk * qk, ind_ref[...])
    r = lax.rsqrt(ssq / DH_B + EPS)
    qk = qk * _dot2(r, indt_ref[...]) * nrm_ref[...]
    for s in range(4):
        slab = _rope128(qk[:, s * LANES:(s + 1) * LANES], c, sa, sb, half)
        q_ref[0, :, s * LANES:(s + 1) * LANES] = (slab * (DH_B ** -0.5)).astype(BF)
    k = _rope128(qk[:, 4 * LANES:5 * LANES], c, sa, sb, half)
    k_ref[0] = k
    kb_ref[0] = k.astype(BF)
    v = z[:, E_V:E_QI]
    v_ref[0] = v
    vb_ref[0] = v.astype(BF)
    for s in range(2):
        lo = E_QI + s * LANES
        hi, lw = _split2(_rope128(z[:, lo:lo + LANES], c, sa, sb, half))
        qih_ref[0, :, s * LANES:(s + 1) * LANES] = hi
        qil_ref[0, :, s * LANES:(s + 1) * LANES] = lw
    ki = _rope128(z[:, E_KI:E_WI], c, sa, sb, half)
    ki_ref[0] = ki[:, :D_IDX]
    kih_ref[0], kil_ref[0] = _split2(ki)
    wi_ref[0] = z[:, E_WI:E_END] * IDX_W_SCALE


def _even_weights(w_in, q_norm, k_norm):
    cuts = [0, 2 * A_CH, 2 * A_CH + 512, 2 * A_CH + 640, 2 * A_CH + 768, 2 * A_CH + 1024, 2 * A_CH + 1088]
    glu = w_in[:, cuts[0]:cuts[1]]
    q = w_in[:, cuts[1]:cuts[2]]
    k = w_in[:, cuts[2]:cuts[3]]
    v = w_in[:, cuts[3]:cuts[4]]
    qi = w_in[:, cuts[4]:cuts[5]]
    ki = w_in[:, cuts[5]:cuts[6]]
    wi = w_in[:, cuts[6]:]
    q = q[:, _DSA_HEAD_PERM]
    pad = jnp.zeros((w_in.shape[0], LANES - H_IDX), w_in.dtype)
    w = jnp.concatenate([glu, q, k, v, qi, ki, ki, wi, pad], axis=1).astype(BF)
    nrm = jnp.concatenate([jnp.tile(q_norm, HB), jnp.tile(k_norm, HKV_B)])[None, :]
    return w, nrm


def _dsa_head_perm():
    j = np.arange(HB * DH_B)
    head = (j // LANES) + (HB // HKV_B) * ((j % LANES) // DH_B)
    return head * DH_B + (j % DH_B)


_DSA_HEAD_PERM = _dsa_head_perm()


def _even_proj(x, g, w, nrm, tables, tb):
    grp, t, _ = x.shape
    ind, indt = _indicator(5 * LANES, DH_B)
    blk = lambda width: pl.BlockSpec((1, tb, width), lambda i, j: (i, j, 0))
    tab = pl.BlockSpec((tb, LANES), lambda i, j: (j, 0))
    sds = lambda width, dt: jax.ShapeDtypeStruct((grp, t, width), dt)
    return pl.pallas_call(
        _even_proj_body,
        grid=(grp, t // tb),
        in_specs=[blk(D_MODEL), _full((1, D_MODEL)), _full(w.shape), _full(nrm.shape),
                  _full(ind.shape), _full(indt.shape), tab, tab, tab],
        out_specs=[blk(A_CH), blk(512), blk(LANES), blk(LANES), blk(D_IDX), blk(LANES),
                   blk(256), blk(256), blk(LANES), blk(LANES), blk(LANES), blk(LANES)],
        out_shape=[sds(A_CH, F32), sds(512, BF), sds(LANES, F32), sds(LANES, F32), sds(D_IDX, F32),
                   sds(LANES, F32), sds(256, BF), sds(256, BF), sds(LANES, BF), sds(LANES, BF),
                   sds(LANES, BF), sds(LANES, BF)],
        compiler_params=_params(("parallel", "parallel")),
        name="even_proj",
    )(x, g[None, :], w, nrm, ind, indt, *tables)


def _conv_tail_body(a_ref, prev_ref, w_ref, b_ref, g_ref, beta_ref, o_ref, buf, shifted, *, tb, pad, stride, rows, carry):
    j = pl.program_id(1)

    @pl.when(j == 0)
    def _():
        buf[0:pad] = prev_ref[0]

    buf[pad:pad + tb] = a_ref[0]
    base = pad - (CONV_A_WIDTH - 1) * stride
    offsets = [base + tap * stride for tap in range(CONV_A_WIDTH)]
    span = pad + tb - SUBLANES
    for r in sorted({o % SUBLANES for o in offsets} - {0}):
        shifted[r - 1, 0:span] = buf[pl.ds(r, span), :]
    for r0 in range(0, tb, rows):
        acc = jnp.broadcast_to(b_ref[...], (rows, A_CH))
        for tap, o in enumerate(offsets):
            r = o % SUBLANES
            src = buf[pl.ds(o + r0, rows), :] if r == 0 else shifted[r - 1, pl.ds(o - r + r0, rows), :]
            acc = acc + w_ref[tap:tap + 1, :] * src
        mu = jnp.mean(acc, axis=-1, keepdims=True)
        d = acc - mu
        var = jnp.mean(d * d, axis=-1, keepdims=True)
        y = d * lax.rsqrt(var + EPS) * g_ref[...] + beta_ref[...]
        o_ref[0, r0:r0 + rows, :] = (y * jax.nn.sigmoid(y)).astype(BF)
    if carry:
        buf[0:pad] = buf[tb:tb + pad]


def _conv_tail(a, prev, w, b, g, beta, tb, stride, rows):
    grp, t, _ = a.shape
    pad = prev.shape[1]
    wp = jnp.concatenate([w, jnp.zeros((1, A_CH), w.dtype)], axis=0)
    n_shift = SUBLANES - 1 if stride % SUBLANES else 1
    body = functools.partial(_conv_tail_body, tb=tb, pad=pad, stride=stride, rows=rows, carry=t > tb)
    return pl.pallas_call(
        body,
        grid=(grp, t // tb),
        in_specs=[pl.BlockSpec((1, tb, A_CH), lambda i, j: (i, j, 0)),
                  pl.BlockSpec((1, pad, A_CH), lambda i, j: (i, 0, 0)),
                  _full(wp.shape), _full((1, A_CH)), _full((1, A_CH)), _full((1, A_CH))],
        out_specs=pl.BlockSpec((1, tb, A_CH), lambda i, j: (i, j, 0)),
        out_shape=jax.ShapeDtypeStruct((grp, t, A_CH), BF),
        scratch_shapes=[pltpu.VMEM((pad + tb, A_CH), F32),
                        pltpu.VMEM((n_shift, pad + tb, A_CH), F32)],
        compiler_params=_params(("parallel", "arbitrary")),
        name="conv_tail",
    )(a, prev, wp, b[None, :], g[None, :], beta[None, :])


def _sort_keys(score):
    score = jnp.where(score == 0.0, 0.0, score)
    bits = pltpu.bitcast(score, jnp.int32)
    return jnp.where(bits < 0, bits ^ jnp.int32(0x7FFFFFFF), bits)


def _kth_largest(key_ref, width, nsel):
    rows = key_ref.shape[0]
    int_min = jnp.int32(-2 ** 31)

    def step(it, thr):
        cand = thr + lax.shift_left(jnp.int32(1), 31 - it)
        cnt = jnp.sum(jnp.where(key_ref[:, 0:width] >= cand, 1.0, 0.0), axis=-1, keepdims=True)
        return jnp.where(cnt >= nsel, cand, thr)

    return lax.fori_loop(0, 32, step, jnp.full((rows, 1), int_min, jnp.int32), unroll=2)


def _select_bias(key_ref, bias_ref, valid, tri_ref, width, nsel, chunk, row_ok=None):
    thr = _kth_largest(key_ref, width, nsel)
    key = key_ref[:, 0:width]
    cnt_gt = jnp.sum(jnp.where(key > thr, 1.0, 0.0), axis=-1, keepdims=True)
    cnt_eq = jnp.sum(jnp.where(key == thr, 1.0, 0.0), axis=-1, keepdims=True)
    need = nsel - cnt_gt
    bias_ref[:, 0:width] = jnp.where(jnp.logical_and(key >= thr, valid), 0.0, NEG)

    surplus = cnt_eq - need
    if row_ok is not None:
        surplus = jnp.where(row_ok, surplus, 0.0)

    @pl.when(jnp.max(surplus) > 0.0)
    def _():
        off = jnp.zeros_like(need)
        for c0 in range(0, width, chunk):
            kc = key_ref[:, c0:c0 + chunk]
            eq = kc == thr
            eqf = jnp.where(eq, 1.0, 0.0)
            rank = _dot(eqf.astype(BF), tri_ref[...]) + off
            take = jnp.logical_or(kc > thr, jnp.logical_and(eq, rank < need))
            bias_ref[:, c0:c0 + chunk] = jnp.where(jnp.logical_and(take, valid[:, c0:c0 + chunk]), 0.0, NEG)
            off = off + jnp.sum(eqf, axis=-1, keepdims=True)


def _strict_upper(n):
    return (jnp.arange(n)[:, None] < jnp.arange(n)[None, :]).astype(BF)


def _stack_heads(x2, low, keep_low):
    zero = jnp.zeros_like(x2)
    return jnp.where(low, x2, zero) if keep_low else jnp.where(low, zero, x2)


DSA_STACK_ROWS = 512


def _dsa_prompt_block(q_ref, qi_ref, wi_ref, k_ref, v_ref, ki_ref, tri_ref, o_ref, key_ref, bias_ref,
                      *, tq, sx, nsel):
    i = pl.program_id(1)
    low = _low_half(tq)
    qpos = i * tq + lax.broadcasted_iota(jnp.int32, (tq, sx), 0)
    kpos = lax.broadcasted_iota(jnp.int32, (tq, sx), 1)
    valid = kpos <= qpos
    per_dot = max(1, DSA_STACK_ROWS // tq)

    def stack(ref, heads, pick_low):
        return jnp.concatenate([_stack_heads(ref[0, :, (h // 2) * LANES:(h // 2 + 1) * LANES], low, pick_low(h))
                                for h in heads], axis=0)

    wi = wi_ref[0]
    score = jnp.zeros((tq, sx), F32)
    for h0 in range(0, H_IDX, per_dot):
        heads = range(h0, min(H_IDX, h0 + per_dot))
        logits = _dot_nt(stack(qi_ref, heads, lambda h: h % 2 == 0), ki_ref[0, 0:sx, :])
        for n, h in enumerate(heads):
            score = score + wi[:, h:h + 1] * jnp.maximum(logits[n * tq:(n + 1) * tq], 0.0)
    score = jnp.where(valid, score, -jnp.inf)
    key_ref[:, 0:sx] = _sort_keys(score)
    _select_bias(key_ref, bias_ref, valid, tri_ref, sx, nsel, LANES)

    bias = bias_ref[:, 0:sx]
    n_pairs = HB // HKV_B
    for p0 in range(0, n_pairs, per_dot):
        pairs = range(p0, min(n_pairs, p0 + per_dot))
        biasn = jnp.concatenate([bias] * len(pairs), axis=0)
        outs = []
        for g in range(HKV_B):
            qs = stack(q_ref, [2 * p for p in pairs], lambda h: g == 0)
            s = _dot_nt(qs, k_ref[0, 0:sx, :]) + biasn
            e = jnp.exp(s - jnp.max(s, axis=-1, keepdims=True))
            l = jnp.sum(e, axis=-1, keepdims=True)
            outs.append(_dot(e.astype(BF), v_ref[0, 0:sx, :]) / l)
        for n, p in enumerate(pairs):
            o_ref[0, :, p * LANES:(p + 1) * LANES] = jnp.where(
                low, outs[0][n * tq:(n + 1) * tq], outs[1][n * tq:(n + 1) * tq]).astype(BF)


MAX_CAUSAL_EXTENTS = 8


def _causal_extents(n_blocks, tq, s):
    nb = max(d for d in range(1, MAX_CAUSAL_EXTENTS + 1) if n_blocks % d == 0)
    per = n_blocks // nb
    return per, [min(s, (c + 1) * per * tq) for c in range(nb)]


def _dsa_prompt_body(*refs, tq, s, nsel):
    per, extents = _causal_extents(s // tq, tq, s)
    i = pl.program_id(1)
    for c, sx in enumerate(extents):
        pl.when(i // per == c)(functools.partial(_dsa_prompt_block, *refs, tq=tq, sx=sx, nsel=nsel))


def _dsa_prompt(q, qi, wi, kb, vb, ki, tq):
    n, s, _ = q.shape
    nsel = min(TOPK_MAX, s // 4)
    tri = _strict_upper(LANES)
    qblk = lambda width: pl.BlockSpec((1, tq, width), lambda i, j: (i, j, 0))
    sblk = lambda width: pl.BlockSpec((1, s, width), lambda i, j: (i, 0, 0))
    return pl.pallas_call(
        functools.partial(_dsa_prompt_body, tq=tq, s=s, nsel=nsel),
        grid=(n, s // tq),
        in_specs=[qblk(512), qblk(256), qblk(LANES), sblk(LANES), sblk(LANES), sblk(LANES), _full(tri.shape)],
        out_specs=qblk(512),
        out_shape=jax.ShapeDtypeStruct((n, s, 512), BF),
        scratch_shapes=[pltpu.VMEM((tq, s), jnp.int32), pltpu.VMEM((tq, s), F32)],
        compiler_params=_params(("parallel", "parallel")),
        name="dsa_prompt",
    )(q, qi, wi, kb, vb, ki, tri)


def _out_proj_body(x_ref, a_ref, b_ref, wa_ref, wb_ref, o_ref):
    o_ref[0] = x_ref[0] + _dot(a_ref[0], wa_ref[...]) + _dot(b_ref[0], wb_ref[...])


def _out_proj(x, a, b, wa, wb, tb):
    grp, t, _ = x.shape
    blk = lambda width: pl.BlockSpec((1, tb, width), lambda i, j: (i, j, 0))
    return pl.pallas_call(
        _out_proj_body,
        grid=(grp, t // tb),
        in_specs=[blk(D_MODEL), blk(a.shape[-1]), blk(b.shape[-1]), _full(wa.shape), _full(wb.shape)],
        out_specs=blk(D_MODEL),
        out_shape=jax.ShapeDtypeStruct(x.shape, F32),
        compiler_params=_params(("parallel", "parallel")),
        name="out_proj",
    )(x, a, b, wa, wb)


FFN_TILE = 256


def _ffn_body(x_ref, g_ref, wu_ref, cw_ref, cb_ref, wd_ref, prev_ref, o_ref, tail_ref,
              hbuf, carry, ubuf, act, *, tb, pad, stride):
    j = pl.program_id(1)
    hbuf[...] = _rms(x_ref[0], g_ref[...]).astype(BF)

    @pl.when(j == 0)
    def _():
        carry[...] = prev_ref[0]

    for f in range(D_FF // FFN_TILE):
        halves = []
        for part in range(2):
            c0 = part * D_FF + f * FFN_TILE
            u = _dot(hbuf[...], wu_ref[:, c0:c0 + FFN_TILE])
            ubuf[0:pad] = carry[:, c0:c0 + FFN_TILE]
            ubuf[pad:pad + tb] = u
            y = (cw_ref[0:1, c0:c0 + FFN_TILE] * ubuf[pl.ds(pad - 2 * stride, tb), :]
                 + cw_ref[1:2, c0:c0 + FFN_TILE] * ubuf[pl.ds(pad - stride, tb), :]
                 + cw_ref[2:3, c0:c0 + FFN_TILE] * u + cb_ref[:, c0:c0 + FFN_TILE])
            carry[:, c0:c0 + FFN_TILE] = ubuf[tb:tb + pad]
            halves.append(y)
        act[:, f * FFN_TILE:(f + 1) * FFN_TILE] = (halves[1] * jax.nn.sigmoid(halves[1]) * halves[0]).astype(BF)
    o_ref[0] = x_ref[0] + _dot(act[...], wd_ref[...])

    @pl.when(j == pl.num_programs(1) - 1)
    def _():
        tail_ref[0] = carry[...]


def _ffn(x, g, wu, cw, cb, wd, prev, tb, stride):
    grp, t, _ = x.shape
    pad = prev.shape[1]
    cwp = jnp.concatenate([cw, jnp.zeros((SUBLANES - FFN_CONV_WIDTH, 2 * D_FF), cw.dtype)], axis=0)
    blk = pl.BlockSpec((1, tb, D_MODEL), lambda i, j: (i, j, 0))
    pblk = pl.BlockSpec((1, pad, 2 * D_FF), lambda i, j: (i, 0, 0))
    once = lambda shape: pl.BlockSpec(shape, lambda i, j: (0,) * len(shape), pipeline_mode=pl.Buffered(1))
    return pl.pallas_call(
        functools.partial(_ffn_body, tb=tb, pad=pad, stride=stride),
        grid=(grp, t // tb),
        in_specs=[blk, _full((1, D_MODEL)), once(wu.shape), _full(cwp.shape), _full((1, 2 * D_FF)),
                  once(wd.shape), pblk],
        out_specs=[blk, pblk],
        out_shape=[jax.ShapeDtypeStruct(x.shape, F32), jax.ShapeDtypeStruct(prev.shape, F32)],
        scratch_shapes=[pltpu.VMEM((tb, D_MODEL), BF), pltpu.VMEM((pad, 2 * D_FF), F32),
                        pltpu.VMEM((pad + tb, FFN_TILE), F32), pltpu.VMEM((tb, D_FF), BF)],
        compiler_params=_params(("parallel", "arbitrary")),
        name="conv_ffn",
    )(x, g[None, :], wu, cwp, cb[None, :], wd, prev)


def _head_norm128(x, gain, ind_ref, indt_ref, dim):
    ssq = _dot2(x * x, ind_ref[...])
    r = lax.rsqrt(ssq / dim + EPS)
    return x * _dot2(r, indt_ref[...]) * gain


def _odd_proj_body(x_ref, g_ref, w_ref, qan_ref, wqb_ref, kvn_ref, qn_ref, kn_ref, ind_ref, indt_ref,
                   c_ref, sa_ref, sb_ref, glg_ref, glb_ref, wuk_ref, wuv_ref,
                   ckv_ref, kpe_ref, u_ref, v_ref, q_ref, k_ref, vv_ref):
    h = _rms(x_ref[0], g_ref[...]).astype(BF)
    z = _dot(h, w_ref[...])
    c, sa, sb = c_ref[...], sa_ref[...], sb_ref[...]
    half = ROPE_C // 2

    qa = _rms(z[:, O_QA:O_CKV], qan_ref[...]).astype(BF)
    q = _head_norm128(_dot(qa, wqb_ref[...]), qn_ref[...], ind_ref, indt_ref, DQK_C)
    for hd in range(HC):
        sl = slice(hd * LANES, (hd + 1) * LANES)
        q_ref[0, :, sl] = (_rope128(q[:, sl], c, sa, sb, half) * MLA_SCALE).astype(BF)

    ckv = _rms(z[:, O_CKV:O_KPE], kvn_ref[...])
    ckv_ref[0] = ckv
    cb = ckv.astype(BF)
    kpe = z[:, O_KPE:O_U]
    kpe_ref[0] = kpe[:, :ROPE_C]
    kfull = _dot(cb, wuk_ref[...]) + jnp.concatenate([kpe] * HC, axis=1)
    k = _head_norm128(kfull, kn_ref[...], ind_ref, indt_ref, DQK_C)
    for hd in range(HC):
        sl = slice(hd * LANES, (hd + 1) * LANES)
        k_ref[0, :, sl] = _rope128(k[:, sl], c, sa, sb, half).astype(BF)
    vv_ref[0] = _dot(cb, wuv_ref[...]).astype(BF)

    zz = jax.nn.gelu(z[:, O_U:O_END])
    u_ref[0] = zz[:, :D_CH]
    vz = zz[:, D_CH:]
    mu = jnp.mean(vz, axis=-1, keepdims=True)
    d = vz - mu
    var = jnp.mean(d * d, axis=-1, keepdims=True)
    v_ref[0] = d * lax.rsqrt(var + EPS) * glg_ref[...] + glb_ref[...]


def _pad_heads(w, lead):
    z32 = jnp.zeros(lead + (HC, ROPE_C), w.dtype)
    return jnp.concatenate([z32, w, z32], axis=-1).reshape(lead + (HC * LANES,))


def _odd_weights(w_in, w_qb, w_uk, w_uv, q_norm, k_norm):
    d = w_in.shape[0]
    cuts = [Q_LORA, Q_LORA + KV_LORA, Q_LORA + KV_LORA + ROPE_C]
    pad = jnp.zeros((d, LANES - ROPE_C), w_in.dtype)
    w = jnp.concatenate([w_in[:, :cuts[2]], pad, w_in[:, cuts[2]:]], axis=1).astype(BF)
    qb = w_qb.reshape(Q_LORA, HC, DQK_C)
    qb = jnp.concatenate([qb, jnp.zeros((Q_LORA, HC, LANES - DQK_C), w_qb.dtype)], axis=-1)
    wqb = qb.reshape(Q_LORA, HC * LANES).astype(BF)
    wuk_pad = _pad_heads(w_uk, (KV_LORA,)).astype(BF)
    wuk = w_uk.reshape(KV_LORA, HC * NOPE_C).astype(BF)
    wuv = w_uv.reshape(KV_LORA, HC * VD_C).astype(BF)
    gain = lambda g: jnp.tile(jnp.concatenate([g, jnp.zeros((LANES - DQK_C,), g.dtype)]), HC)[None, :]
    return w, wqb, wuk_pad, wuk, wuv, gain(q_norm), gain(k_norm)


def _odd_proj(x, g, w, wqb, wuk_pad, wuv, qan, kvn, qn, kn, glg, glb, tables, tb):
    grp, t, _ = x.shape
    ind, indt = _indicator(HC * LANES, LANES)
    blk = lambda width: pl.BlockSpec((1, tb, width), lambda i, j: (i, j, 0))
    tab = pl.BlockSpec((tb, LANES), lambda i, j: (j, 0))
    sds = lambda width, dt: jax.ShapeDtypeStruct((grp, t, width), dt)
    row = lambda v: v[None, :]
    return pl.pallas_call(
        _odd_proj_body,
        grid=(grp, t // tb),
        in_specs=[blk(D_MODEL), _full((1, D_MODEL)), _full(w.shape), _full((1, Q_LORA)), _full(wqb.shape),
                  _full((1, KV_LORA)), _full(qn.shape), _full(kn.shape), _full(ind.shape), _full(indt.shape),
                  tab, tab, tab, _full((1, D_CH)), _full((1, D_CH)), _full(wuk_pad.shape), _full(wuv.shape)],
        out_specs=[blk(KV_LORA), blk(ROPE_C), blk(D_CH), blk(D_CH), blk(HC * LANES), blk(HC * LANES),
                   blk(HC * VD_C)],
        out_shape=[sds(KV_LORA, F32), sds(ROPE_C, F32), sds(D_CH, F32), sds(D_CH, F32),
                   sds(HC * LANES, BF), sds(HC * LANES, BF), sds(HC * VD_C, BF)],
        compiler_params=_params(("parallel", "parallel")),
        name="odd_proj",
    )(x, row(g), w, row(qan), wqb, row(kvn), qn, kn, ind, indt, *tables, row(glg), row(glb), wuk_pad, wuv)


def _gmlp_prompt_body(u_ref, v_ref, w_ref, b_ref, o_ref, *, tb):
    rows = D_GROUPS * CHUNK
    t_in = lax.broadcasted_iota(jnp.int32, (rows, CHUNK), 0) % CHUNK
    s_in = lax.broadcasted_iota(jnp.int32, (rows, CHUNK), 1)
    w = jnp.where(s_in <= t_in, w_ref[...], 0.0).astype(BF)
    grp = lax.broadcasted_iota(jnp.int32, (CHUNK, D_CH), 1) // (D_CH // D_GROUPS)
    for c0 in range(0, tb, CHUNK):
        y = _dot(w, v_ref[0, c0:c0 + CHUNK, :].astype(BF))
        mix = b_ref[...]
        for gi in range(D_GROUPS):
            mix = mix + jnp.where(grp == gi, y[gi * CHUNK:(gi + 1) * CHUNK], 0.0)
        o_ref[0, c0:c0 + CHUNK, :] = (u_ref[0, c0:c0 + CHUNK, :] * mix).astype(BF)


def _gmlp_prompt(u, v, w_s, b_s, tb):
    n, s, _ = u.shape
    w = w_s.reshape(D_GROUPS * CHUNK, CHUNK)
    b = jnp.repeat(b_s.T, D_CH // D_GROUPS, axis=1)
    blk = pl.BlockSpec((1, tb, D_CH), lambda i, j: (i, j, 0))
    return pl.pallas_call(
        functools.partial(_gmlp_prompt_body, tb=tb),
        grid=(n, s // tb),
        in_specs=[blk, blk, _full(w.shape), _full(b.shape)],
        out_specs=blk,
        out_shape=jax.ShapeDtypeStruct(u.shape, BF),
        compiler_params=_params(("parallel", "parallel")),
        name="gmlp_prompt",
    )(u, v, w, b)


def _gmlp_sample_body(u_ref, v_ref, w_ref, b_ref, o_ref, *, nb, nt):
    for t in range(nt):
        mix = jnp.broadcast_to(b_ref[t:t + 1, :], (nb, D_CH))
        for s in range(t + 1):
            mix = mix + w_ref[t * nt + s:t * nt + s + 1, :] * v_ref[s * nb:(s + 1) * nb, :]
        o_ref[t * nb:(t + 1) * nb, :] = (u_ref[t * nb:(t + 1) * nb, :] * mix).astype(BF)


def _gmlp_sample(u, v, w_s, b_s, nb, nt):
    lane = D_CH // D_GROUPS
    w = jnp.repeat(jnp.transpose(w_s[:, :nt, :nt], (1, 2, 0)).reshape(nt * nt, D_GROUPS), lane, axis=1)
    b = jnp.repeat(b_s.T[:nt], lane, axis=1)
    return pl.pallas_call(
        functools.partial(_gmlp_sample_body, nb=nb, nt=nt),
        out_shape=jax.ShapeDtypeStruct(u.shape, BF),
        name="gmlp_sample",
    )(u, v, w, b)


def _mla_prompt_block(q_ref, k_ref, v_ref, o_ref, *, tq, sx):
    i = pl.program_id(1)
    low = _low_half(tq)
    qpos = i * tq + lax.broadcasted_iota(jnp.int32, (tq, sx), 0)
    kpos = lax.broadcasted_iota(jnp.int32, (tq, sx), 1)
    bias = jnp.where(kpos <= qpos, 0.0, NEG)
    for p in range(HC // 2):
        outs = []
        for hd in (2 * p, 2 * p + 1):
            sl = slice(hd * LANES, (hd + 1) * LANES)
            s = _dot_nt(q_ref[0, :, sl], k_ref[0, 0:sx, sl]) + bias
            e = jnp.exp(s - jnp.max(s, axis=-1, keepdims=True))
            l = jnp.sum(e, axis=-1, keepdims=True)
            outs.append(_dot(e.astype(BF), v_ref[0, 0:sx, p * LANES:(p + 1) * LANES]) / l)
        o_ref[0, :, p * LANES:(p + 1) * LANES] = jnp.where(low, outs[0], outs[1]).astype(BF)


def _mla_prompt_body(*refs, tq, s):
    per, extents = _causal_extents(s // tq, tq, s)
    i = pl.program_id(1)
    for c, sx in enumerate(extents):
        pl.when(i // per == c)(functools.partial(_mla_prompt_block, *refs, tq=tq, sx=sx))


def _mla_prompt(q, k, v, tq):
    n, s, _ = q.shape
    return pl.pallas_call(
        functools.partial(_mla_prompt_body, tq=tq, s=s),
        grid=(n, s // tq),
        in_specs=[pl.BlockSpec((1, tq, HC * LANES), lambda i, j: (i, j, 0)),
                  pl.BlockSpec((1, s, HC * LANES), lambda i, j: (i, 0, 0)),
                  pl.BlockSpec((1, s, HC * VD_C), lambda i, j: (i, 0, 0))],
        out_specs=pl.BlockSpec((1, tq, HC * VD_C), lambda i, j: (i, j, 0)),
        out_shape=jax.ShapeDtypeStruct((n, s, HC * VD_C), BF),
        compiler_params=_params(("parallel", "parallel")),
        name="mla_prompt",
    )(q, k, v)


PAGES_PER_STEP = 32


def _page_specs(cache_shape, layer):
    _, _, rows, width = cache_shape

    def spec(slot):
        return pl.BlockSpec((None, None, rows, width),
                            lambda s, j, pt: (layer, pt[s, j * PAGES_PER_STEP + slot], 0, 0))

    return [spec(slot) for slot in range(PAGES_PER_STEP)]


def _cat_pages(refs, axis):
    return jnp.concatenate([r[...] for r in refs], axis=axis)


def _idx_scores(qh, ql, wi, kh, kl):
    both = _dot(jnp.concatenate([qh, ql], axis=0), kh)
    rows = qh.shape[0]
    logits = both[0:rows] + both[rows:2 * rows] + _dot(qh, kl)
    score = jnp.zeros((SUBLANES, logits.shape[1]), F32)
    for h in range(H_IDX):
        sl = slice(h * SUBLANES, (h + 1) * SUBLANES)
        score = score + wi[sl, 0:1] * jnp.maximum(logits[sl], 0.0)
    return score


def _dsa_sample_score_body(pt_ref, qh_ref, ql_ref, wi_ref, *rest):
    del pt_ref
    pages, o_ref = rest[:PAGES_PER_STEP], rest[PAGES_PER_STEP]
    kh, kl = _split2(_cat_pages(pages, 1))
    o_ref[0] = _idx_scores(qh_ref[0], ql_ref[0], wi_ref[0], kh, kl)


def _dsa_sample_scores(page_table, qh, ql, wi, cache_ki, layer):
    n, n_pages = page_table.shape
    steps = n_pages // PAGES_PER_STEP
    span = PAGES_PER_STEP * PAGE_SIZE
    per_seq = lambda shape: pl.BlockSpec((1,) + shape, lambda s, j, pt: (s, 0, 0))
    grid_spec = pltpu.PrefetchScalarGridSpec(
        num_scalar_prefetch=1,
        grid=(n, steps),
        in_specs=[per_seq((4 * SUBLANES, D_IDX)), per_seq((4 * SUBLANES, D_IDX)), per_seq((4 * SUBLANES, LANES))]
        + _page_specs(cache_ki.shape, layer),
        out_specs=pl.BlockSpec((1, SUBLANES, span), lambda s, j, pt: (s, 0, j)),
    )
    return pl.pallas_call(
        _dsa_sample_score_body,
        grid_spec=grid_spec,
        out_shape=jax.ShapeDtypeStruct((n, SUBLANES, n_pages * PAGE_SIZE), F32),
        compiler_params=_params(("parallel", "arbitrary")),
        name="dsa_sample_scores",
    )(page_table, qh, ql, wi, *([cache_ki] * PAGES_PER_STEP))


SELECT_SEQS = 8


def _dsa_sample_select_body(score_ref, qh_ref, ql_ref, wi_ref, knh_ref, knl_ref, tri_ref, bias_ref, key_ref,
                            *, past, nsel, nt):
    width = past + PAGE_SIZE
    rows = SELECT_SEQS * SUBLANES
    slot = lax.broadcasted_iota(jnp.int32, (rows, width), 0) % SUBLANES
    kpos = lax.broadcasted_iota(jnp.int32, (rows, width), 1)
    valid = kpos <= past + slot
    for g in range(SELECT_SEQS):
        sl = slice(g * SUBLANES, (g + 1) * SUBLANES)
        new = _idx_scores(qh_ref[g], ql_ref[g], wi_ref[g], knh_ref[g], knl_ref[g])
        key_ref[sl, 0:past] = _sort_keys(score_ref[g])
        key_ref[sl, past:width] = _sort_keys(jnp.where(valid[sl, past:width], new, -jnp.inf))
    row_ok = lax.broadcasted_iota(jnp.int32, (rows, 1), 0) % SUBLANES < nt
    _select_bias(key_ref, bias_ref, valid, tri_ref, width, nsel, LANES, row_ok)


def _dsa_sample_select(score, qh, ql, wi, knh, knl, nt):
    n, _, past = score.shape
    nsel = min(TOPK_MAX, (past + nt) // 4)
    tri = _strict_upper(LANES)
    width = past + PAGE_SIZE
    rows = SELECT_SEQS * SUBLANES
    blk = lambda a: pl.BlockSpec((SELECT_SEQS,) + a.shape[1:], lambda i: (i, 0, 0))
    return pl.pallas_call(
        functools.partial(_dsa_sample_select_body, past=past, nsel=nsel, nt=nt),
        grid=(n // SELECT_SEQS,),
        in_specs=[blk(score), blk(qh), blk(ql), blk(wi), blk(knh), blk(knl), _full(tri.shape)],
        out_specs=pl.BlockSpec((rows, width), lambda i: (i, 0)),
        out_shape=jax.ShapeDtypeStruct((n * SUBLANES, width), F32),
        scratch_shapes=[pltpu.VMEM((rows, width), jnp.int32)],
        compiler_params=_params(("parallel",)),
        name="dsa_sample_select",
    )(score, qh, ql, wi, knh, knl, tri)


def _dsa_sample_attend_body(pt_ref, bias_ref, btail_ref, q_ref, kn_ref, vn_ref, *rest):
    del pt_ref
    kpages = rest[:PAGES_PER_STEP]
    vpages = rest[PAGES_PER_STEP:2 * PAGES_PER_STEP]
    o_ref, q64_ref, m_ref, l_ref, acc_ref = rest[2 * PAGES_PER_STEP:]
    j = pl.program_id(1)
    rows8 = SUBLANES
    per = HB // HKV_B

    @pl.when(j == 0)
    def _():
        low = _low_half(rows8)
        q8 = q_ref[0].astype(F32)
        q64_ref[...] = jnp.concatenate(
            [_stack_heads(q8[:, (h % per) * LANES:(h % per + 1) * LANES], low, h < per) for h in range(HB)],
            axis=0).astype(BF)
        m_ref[...] = jnp.full(m_ref.shape, NEG, F32)
        l_ref[...] = jnp.zeros(l_ref.shape, F32)
        acc_ref[...] = jnp.zeros(acc_ref.shape, F32)

    def fold(bias8, k_t, v_t):
        s = _dot(q64_ref[...], k_t) + jnp.concatenate([bias8] * HB, axis=0)
        m_old = m_ref[...]
        m_new = jnp.maximum(m_old, jnp.max(s, axis=-1, keepdims=True))
        alpha = jnp.exp(m_old - m_new)
        e = jnp.exp(s - m_new)
        l_ref[...] = alpha * l_ref[...] + jnp.sum(e, axis=-1, keepdims=True)
        acc_ref[...] = alpha * acc_ref[...] + _dot_nt(e.astype(BF), v_t)
        m_ref[...] = m_new

    fold(bias_ref[0], _cat_pages(kpages, 1).astype(BF), _cat_pages(vpages, 1).astype(BF))

    @pl.when(j == pl.num_programs(1) - 1)
    def _():
        fold(btail_ref[0], kn_ref[0], vn_ref[0])
        out = acc_ref[...] / l_ref[...]
        low = _low_half(rows8)
        for p in range(per):
            o_ref[0, :, p * LANES:(p + 1) * LANES] = jnp.where(
                low, out[p * rows8:(p + 1) * rows8], out[(per + p) * rows8:(per + p + 1) * rows8]).astype(BF)


def _dsa_sample_attend(page_table, bias, q8, kn, vn, cache_k, cache_v, layer):
    n, n_pages = page_table.shape
    span = PAGES_PER_STEP * PAGE_SIZE
    per_seq = lambda shape: pl.BlockSpec((1,) + shape, lambda s, j, pt: (s, 0, 0))
    grid_spec = pltpu.PrefetchScalarGridSpec(
        num_scalar_prefetch=1,
        grid=(n, n_pages // PAGES_PER_STEP),
        in_specs=[pl.BlockSpec((1, SUBLANES, span), lambda s, j, pt: (s, 0, j)),
                  pl.BlockSpec((1, SUBLANES, PAGE_SIZE), lambda s, j, pt: (s, 0, n_pages)),
                  per_seq((SUBLANES, HB * DH_B)), per_seq((LANES, PAGE_SIZE)), per_seq((LANES, PAGE_SIZE))]
        + _page_specs(cache_k.shape, layer) + _page_specs(cache_v.shape, layer),
        out_specs=per_seq((SUBLANES, HB * DH_B)),
        scratch_shapes=[pltpu.VMEM((HB * SUBLANES, LANES), BF), pltpu.VMEM((HB * SUBLANES, 1), F32),
                        pltpu.VMEM((HB * SUBLANES, 1), F32), pltpu.VMEM((HB * SUBLANES, LANES), F32)],
    )
    return pl.pallas_call(
        _dsa_sample_attend_body,
        grid_spec=grid_spec,
        out_shape=jax.ShapeDtypeStruct((n, SUBLANES, HB * DH_B), BF),
        compiler_params=_params(("parallel", "arbitrary")),
        name="dsa_sample_attend",
    )(page_table, bias, bias, q8, kn, vn, *([cache_k] * PAGES_PER_STEP), *([cache_v] * PAGES_PER_STEP))


MLA_ROWS = HC * SUBLANES


def _mla_sample_body(pt_ref, q_ref, kn_ref, vn_ref, wukp_ref, wukt_ref, wuv_ref, gk_ref, ct_ref, st_ref,
                     *rest, nt):
    del pt_ref
    cpages = rest[:PAGES_PER_STEP]
    ppages = rest[PAGES_PER_STEP:2 * PAGES_PER_STEP]
    o_ref, qbd_ref, lhs_ref, qrot_ref, m_ref, l_ref, acc_ref = rest[2 * PAGES_PER_STEP:]
    j = pl.program_id(1)
    rows = MLA_ROWS
    feat = HC * NOPE_C

    @pl.when(j == 0)
    def _():
        q8 = q_ref[0].astype(F32)
        rhead = lax.broadcasted_iota(jnp.int32, (rows, HC * LANES), 0) // SUBLANES
        lhead = lax.broadcasted_iota(jnp.int32, (rows, HC * LANES), 1) // LANES
        qbd = jnp.where(rhead == lhead, jnp.concatenate([q8] * HC, axis=0), 0.0)
        qbd_ref[...] = qbd.astype(BF)
        lhs_ref[0:feat, :] = wukt_ref[...]
        lhs_ref[feat:feat + rows, :] = _dot_nt((qbd * gk_ref[...]).astype(BF), wukp_ref[...]).astype(BF)
        folded = qbd[:, 0:LANES]
        for hd in range(1, HC):
            folded = folded + qbd[:, hd * LANES:(hd + 1) * LANES]
        lane = lax.broadcasted_iota(jnp.int32, (rows, LANES), 1)
        half = ROPE_C // 2
        swapped = jnp.where(lane < half, pltpu.roll(folded, LANES - half, 1),
                            jnp.where(lane < ROPE_C, -pltpu.roll(folded, half, 1), 0.0))
        gpe = gk_ref[:, 0:LANES]
        qrot_ref[...] = jnp.concatenate([(folded * gpe)[:, 0:ROPE_C], (swapped * gpe)[:, 0:ROPE_C]],
                                        axis=1).astype(BF)
        m_ref[...] = jnp.full(m_ref.shape, NEG, F32)
        l_ref[...] = jnp.zeros(l_ref.shape, F32)
        acc_ref[...] = jnp.zeros(acc_ref.shape, F32)

    cb = _cat_pages(cpages, 0).astype(BF)
    kpe = _cat_pages(ppages, 1)
    big = _dot_nt(lhs_ref[...], cb)
    kp2 = jnp.sum(kpe * kpe, axis=0, keepdims=True)
    rot = _dot(qrot_ref[...], jnp.concatenate([kpe * ct_ref[...], kpe * st_ref[...]], axis=0).astype(BF))
    parts = []
    for hd in range(HC):
        kn = big[hd * NOPE_C:(hd + 1) * NOPE_C]
        r = lax.rsqrt((jnp.sum(kn * kn, axis=0, keepdims=True) + kp2) / DQK_C + EPS)
        sl = slice(hd * SUBLANES, (hd + 1) * SUBLANES)
        parts.append(r * (big[feat + hd * SUBLANES:feat + (hd + 1) * SUBLANES] + rot[sl]))
    s = jnp.concatenate(parts, axis=0)

    m_old = m_ref[...]
    m_new = jnp.maximum(m_old, jnp.max(s, axis=-1, keepdims=True))
    alpha = jnp.exp(m_old - m_new)
    e = jnp.exp(s - m_new)
    l_ref[...] = alpha * l_ref[...] + jnp.sum(e, axis=-1, keepdims=True)
    acc_ref[...] = alpha * acc_ref[...] + _dot(e.astype(BF), cb)
    m_ref[...] = m_new

    @pl.when(j == pl.num_programs(1) - 1)
    def _():
        s_new = _dot_nt(qbd_ref[...], kn_ref[0])
        t_row = lax.broadcasted_iota(jnp.int32, s_new.shape, 0) % SUBLANES
        col = lax.broadcasted_iota(jnp.int32, s_new.shape, 1)
        s_new = jnp.where(jnp.logical_and(col <= t_row, col < nt), s_new, NEG)
        m_old = m_ref[...]
        m_fin = jnp.maximum(m_old, jnp.max(s_new, axis=-1, keepdims=True))
        alpha = jnp.exp(m_old - m_fin)
        e = jnp.exp(s_new - m_fin)
        l = alpha * l_ref[...] + jnp.sum(e, axis=-1, keepdims=True)
        out = (_dot((alpha * acc_ref[...]).astype(BF), wuv_ref[...]) + _dot(e.astype(BF), vn_ref[0])) / l
        lhead = lax.broadcasted_iota(jnp.int32, (SUBLANES, HC * VD_C), 1) // VD_C
        res = jnp.zeros((SUBLANES, HC * VD_C), F32)
        for hd in range(HC):
            res = res + jnp.where(lhead == hd, out[hd * SUBLANES:(hd + 1) * SUBLANES], 0.0)
        o_ref[0] = res.astype(BF)


def _mla_sample(page_table, q8, kn, vn, wuk_pad, wuk, wuv, gk, ctab, stab, cache_ckv, cache_kpe, layer, nt):
    n, n_pages = page_table.shape
    span = PAGES_PER_STEP * PAGE_SIZE
    rows = MLA_ROWS
    wukt = wuk.T
    per_seq = lambda shape: pl.BlockSpec((1,) + shape, lambda s, j, pt: (s, 0, 0))
    const = lambda a: pl.BlockSpec(a.shape, lambda s, j, pt: (0,) * a.ndim)
    tab = pl.BlockSpec((ROPE_C, span), lambda s, j, pt: (0, j))
    grid_spec = pltpu.PrefetchScalarGridSpec(
        num_scalar_prefetch=1,
        grid=(n, n_pages // PAGES_PER_STEP),
        in_specs=[per_seq((SUBLANES, HC * LANES)), per_seq((PAGE_SIZE, HC * LANES)), per_seq((PAGE_SIZE, HC * VD_C)),
                  const(wuk_pad), const(wukt), const(wuv), const(gk), tab, tab]
        + _page_specs(cache_ckv.shape, layer) + _page_specs(cache_kpe.shape, layer),
        out_specs=per_seq((SUBLANES, HC * VD_C)),
        scratch_shapes=[pltpu.VMEM((rows, HC * LANES), BF), pltpu.VMEM((HC * NOPE_C + rows, KV_LORA), BF),
                        pltpu.VMEM((rows, 2 * ROPE_C), BF),
                        pltpu.VMEM((rows, 1), F32), pltpu.VMEM((rows, 1), F32), pltpu.VMEM((rows, KV_LORA), F32)],
    )
    return pl.pallas_call(
        functools.partial(_mla_sample_body, nt=nt),
        grid_spec=grid_spec,
        out_shape=jax.ShapeDtypeStruct((n, SUBLANES, HC * VD_C), BF),
        compiler_params=_params(("parallel", "arbitrary")),
        name="mla_sample",
    )(page_table, q8, kn, vn, wuk_pad, wukt, wuv, gk, ctab, stab,
      *([cache_ckv] * PAGES_PER_STEP), *([cache_kpe] * PAGES_PER_STEP))


PROMPT_ROWS = 512
CONV_ROWS = 256
DSA_QBLOCK = 256
MLA_QBLOCK = 512


def kernel(x_prompt, x_sample, cache_dsa_k, cache_dsa_v, cache_dsa_kidx, state_conv_a, cache_mla_ckv, cache_mla_kpe, state_ffn_conv, page_table, norm_mix, norm_ffn, w_in_e, conv_a_w, conv_a_b, conv_a_ln_g, conv_a_ln_b, q_norm_b, k_norm_b, w_out_e, w_in_o, q_a_norm, w_qb, kv_a_norm, w_uk, w_uv, q_norm_c, k_norm_c, gmlp_ln_g, gmlp_ln_b, w_spatial, b_spatial, w_out_o, w_up, ffn_conv_w, ffn_conv_b, w_down):
    bsz, seq, _ = x_prompt.shape
    nb, nt, _ = x_sample.shape
    depth = norm_mix.shape[0]
    past = page_table.shape[1] * PAGE_SIZE
    srows = nt * nb
    pos_p = jnp.arange(seq)
    pos_s = past + jnp.repeat(jnp.arange(nt), nb)

    def to_rows(a):
        return jnp.transpose(a, (1, 0, 2)).reshape(1, a.shape[1] * nb, a.shape[2])

    def to_seq(a):
        return jnp.transpose(a.reshape(-1, nb, a.shape[-1]), (1, 0, 2))

    def pad_rows(a, rows):
        return jnp.pad(a, ((0, 0), (0, rows - a.shape[1]), (0, 0)))

    def idx_stack(a):
        w = a.shape[-1] // H_IDX
        a = jnp.transpose(to_seq(a).reshape(nb, nt, H_IDX, w), (0, 2, 1, 3))
        a = jnp.pad(a, ((0, 0), (0, 0), (0, SUBLANES - nt), (0, 0)))
        return a.reshape(nb, H_IDX * SUBLANES, w)

    yp = x_prompt
    ys = to_rows(x_sample)
    pk, pv, pki, pca, pckv, pkpe, pff = [], [], [], [], [], [], []
    sk, sv, ski, sca, sckv, skpe, sgv, sff = [], [], [], [], [], [], [], []

    for layer in range(depth):
        if layer % 2 == 0:
            e = layer // 2
            w, nrm = _even_weights(w_in_e[e], q_norm_b[e], k_norm_b[e])
            tabs_p = _rope_tables(pos_p, ROT_B, DH_B)
            tabs_s = _rope_tables(pos_s, ROT_B, DH_B)
            ap, qp, kp, vp, kip, wip, qihp, qilp, kbp, vbp, kihp, kilp = _even_proj(
                yp, norm_mix[layer], w, nrm, tabs_p, PROMPT_ROWS)
            a_s, qs, ks, vs, kis, wis, qihs, qils, kbs, vbs, kihs, kils = _even_proj(
                ys, norm_mix[layer], w, nrm, tabs_s, srows)

            hist = CONV_A_WIDTH - 1
            conv = (conv_a_w[e], conv_a_b[e], conv_a_ln_g[e], conv_a_ln_b[e])
            cp = _conv_tail(ap, jnp.zeros((bsz, 32, A_CH), F32), *conv, CONV_ROWS, 1, 32)
            cs = _conv_tail(a_s, to_rows(state_conv_a[e]), *conv, srows, nb, nb)
            pca.append(ap[:, seq - hist:, :])
            sca.append(jnp.concatenate([state_conv_a[e], to_seq(a_s)], axis=1)[:, nt:, :])

            bp = _dsa_prompt(qp, qihp, wip, kbp, vbp, kihp, DSA_QBLOCK)

            qh32, ql32 = idx_stack(qihs), idx_stack(qils)
            wi32 = jnp.broadcast_to(idx_stack(wis[..., :H_IDX]), (nb, H_IDX * SUBLANES, LANES))
            n_pool = cache_dsa_k.shape[1]
            kv_t = lambda c: jnp.transpose(c, (0, 1, 3, 4, 2)).reshape(-1, n_pool, HKV_B * DH_B, PAGE_SIZE)
            new_t = lambda a: jnp.transpose(pad_rows(to_seq(a), PAGE_SIZE), (0, 2, 1))
            scores = _dsa_sample_scores(page_table, qh32, ql32, wi32, jnp.transpose(cache_dsa_kidx, (0, 1, 3, 2)), e)
            bias = _dsa_sample_select(scores, qh32, ql32, wi32, new_t(kihs[..., :D_IDX]), new_t(kils[..., :D_IDX]), nt)
            bs8 = _dsa_sample_attend(page_table, bias.reshape(nb, SUBLANES, -1), pad_rows(to_seq(qs), SUBLANES),
                                     new_t(kbs), new_t(vbs), kv_t(cache_dsa_k), kv_t(cache_dsa_v), e)
            bs = to_rows(bs8[:, :nt])

            wa = w_out_e[e][:A_CH].astype(BF)
            wb = w_out_e[e][A_CH:][_DSA_HEAD_PERM].astype(BF)
            yp = _out_proj(yp, cp, bp, wa, wb, PROMPT_ROWS)
            ys = _out_proj(ys, cs, bs, wa, wb, srows)

            pk.append(kp.reshape(bsz, seq, HKV_B, DH_B))
            pv.append(vp.reshape(bsz, seq, HKV_B, DH_B))
            pki.append(kip)
            sk.append(to_seq(ks).reshape(nb, nt, HKV_B, DH_B))
            sv.append(to_seq(vs).reshape(nb, nt, HKV_B, DH_B))
            ski.append(to_seq(kis))
        else:
            o = layer // 2
            w, wqb, wukp, wuk, wuv, qn, kn = _odd_weights(w_in_o[o], w_qb[o], w_uk[o], w_uv[o],
                                                          q_norm_c[o], k_norm_c[o])
            tabs_p = _rope_tables(pos_p, ROPE_C, LANES)
            tabs_s = _rope_tables(pos_s, ROPE_C, LANES)
            rest = (w, wqb, wukp, wuv, q_a_norm[o], kv_a_norm[o], qn, kn, gmlp_ln_g[o], gmlp_ln_b[o])
            ckvp, kpep, up, vp, q_p, k_p, v_p = _odd_proj(yp, norm_mix[layer], *rest, tabs_p, PROMPT_ROWS)
            ckvs, kpes, us, vs, q_s, k_s, v_s = _odd_proj(ys, norm_mix[layer], *rest, tabs_s, srows)

            mp = _mla_prompt(q_p, k_p, v_p, MLA_QBLOCK)
            gp = _gmlp_prompt(up, vp, w_spatial[o], b_spatial[o], PROMPT_ROWS)
            gs = _gmlp_sample(us[0], vs[0], w_spatial[o], b_spatial[o], nb, nt)[None]

            half = ROPE_C // 2
            inv_freq = jnp.power(jnp.float32(ROPE_THETA), -jnp.arange(half, dtype=F32) * (2.0 / ROPE_C))
            ang = jnp.arange(past).astype(F32)[:, None] * inv_freq[None, :]
            ctab = jnp.tile(jnp.cos(ang), (1, 2)).T
            stab = jnp.tile(jnp.sin(ang), (1, 2)).T
            ms8 = _mla_sample(page_table, pad_rows(to_seq(q_s), SUBLANES), pad_rows(to_seq(k_s), PAGE_SIZE),
                              pad_rows(to_seq(v_s), PAGE_SIZE), wukp, wuk, wuv, kn, ctab, stab,
                              cache_mla_ckv, jnp.transpose(cache_mla_kpe, (0, 1, 3, 2)), o, nt)
            ms = to_rows(ms8[:, :nt])

            wa = w_out_o[o][:HC * VD_C].astype(BF)
            wb = w_out_o[o][HC * VD_C:].astype(BF)
            yp = _out_proj(yp, mp, gp, wa, wb, PROMPT_ROWS)
            ys = _out_proj(ys, ms, gs, wa, wb, srows)

            pckv.append(ckvp)
            pkpe.append(kpep)
            sckv.append(to_seq(ckvs))
            skpe.append(to_seq(kpes))
            sgv.append(to_seq(vs))

        keep = FFN_CONV_WIDTH - 1
        ffn = (norm_ffn[layer], w_up[layer].astype(BF), ffn_conv_w[layer], ffn_conv_b[layer],
               w_down[layer].astype(BF))
        yp, tail_p = _ffn(yp, *ffn, jnp.zeros((bsz, SUBLANES, 2 * D_FF), F32), PROMPT_ROWS, 1)
        ys, tail_s = _ffn(ys, *ffn, to_rows(state_ffn_conv[layer]), srows, nb)
        pff.append(tail_p[:, SUBLANES - keep:, :])
        sff.append(to_seq(tail_s))

    return (yp, to_seq(ys),
            jnp.stack(pk), jnp.stack(pv), jnp.stack(pki), jnp.stack(pca),
            jnp.stack(pckv), jnp.stack(pkpe), jnp.stack(pff),
            jnp.stack(sk), jnp.stack(sv), jnp.stack(ski), jnp.stack(sca),
            jnp.stack(sckv), jnp.stack(skpe), jnp.stack(sgv), jnp.stack(sff))
```

```python
import functools

import numpy as np
import jax
import jax.numpy as jnp
from jax import lax
from jax.experimental import pallas as pl
from jax.experimental.pallas import tpu as pltpu

F32 = jnp.float32
BF = jnp.bfloat16

D_MODEL = 1024
PAGE_SIZE = 128
ROPE_THETA = 500000.0
EPS = 1e-6

A_CH = D_MODEL // 2
CONV_A_WIDTH = 31

DH_B = 64
HB = (D_MODEL // 2) // DH_B
HKV_B = 2
ROT_B = DH_B // 4
H_IDX = 4
D_IDX = 64
ROT_IDX = D_IDX // 4
TOPK_MAX = 256
IDX_W_SCALE = (H_IDX * D_IDX) ** -0.5

VD_C = 64
HC = (D_MODEL // 2) // VD_C
NOPE_C = 64
ROPE_C = 32
DQK_C = NOPE_C + ROPE_C
Q_LORA = 3 * D_MODEL // 8
KV_LORA = D_MODEL // 4
MLA_SCALE = DQK_C ** -0.5

D_CH = D_MODEL // 2
D_GROUPS = 8
CHUNK = 128

D_FF = 11 * D_MODEL // 4
FFN_CONV_WIDTH = 3

LANES = 128
SUBLANES = 8
NEG = -1e30
VMEM_LIMIT = 56 * 1024 * 1024

E_A, E_G, E_Q, E_K, E_V, E_QI, E_KI, E_WI, E_END = 0, 512, 1024, 1536, 1664, 1792, 2048, 2176, 2304
O_QA, O_CKV, O_KPE, O_U, O_V, O_END = 0, 384, 640, 768, 1280, 1792


def _dot(a, b):
    return jnp.dot(a, b, preferred_element_type=F32)


def _dot_nt(a, b):
    return lax.dot_general(a, b, (((1,), (1,)), ((), ())), preferred_element_type=F32)


def _split2(x):
    hi = x.astype(BF)
    lo = (x - hi.astype(F32)).astype(BF)
    return hi, lo


def _dot2(x, m):
    hi, lo = _split2(x)
    return _dot(hi, m) + _dot(lo, m)


def _rms(x, g):
    return x * lax.rsqrt(jnp.mean(x * x, axis=-1, keepdims=True) + EPS) * g


def _rope128(x, c, sa, sb, half):
    return x * c + pltpu.roll(x, LANES - half, 1) * sa + pltpu.roll(x, half, 1) * sb


def _params(sem, vmem=VMEM_LIMIT):
    return pltpu.CompilerParams(dimension_semantics=sem, vmem_limit_bytes=vmem)


def _full(shape):
    n = len(shape)
    return pl.BlockSpec(shape, lambda *_: (0,) * n)


def _low_half(rows):
    return lax.broadcasted_iota(jnp.int32, (rows, LANES), 1) < (LANES // 2)


def _rope_tables(pos, n_rot, head_w):
    half = n_rot // 2
    inv_freq = jnp.power(jnp.float32(ROPE_THETA), -jnp.arange(half, dtype=F32) * (2.0 / n_rot))
    ang = pos.astype(F32)[:, None] * inv_freq[None, :]
    cos, sin = jnp.cos(ang), jnp.sin(ang)
    m = pos.shape[0]
    one = jnp.ones((m, head_w - n_rot), F32)
    zero = jnp.zeros((m, head_w - n_rot), F32)
    zh = jnp.zeros((m, half), F32)
    c = jnp.concatenate([cos, cos, one], axis=1)
    sa = jnp.concatenate([-sin, zh, zero], axis=1)
    sb = jnp.concatenate([zh, sin, zero], axis=1)
    rep = LANES // head_w
    return tuple(jnp.tile(t, (1, rep)) for t in (c, sa, sb))


def _indicator(width, group):
    lane = jnp.arange(width)[:, None] // group
    col = jnp.arange(LANES)[None, :]
    ind = (lane == col).astype(BF)
    return ind, ind.T


def _even_proj_body(x_ref, g_ref, w_ref, nrm_ref, ind_ref, indt_ref, c_ref, sa_ref, sb_ref,
                    a_ref, q_ref, k_ref, v_ref, ki_ref, wi_ref, qih_ref, qil_ref, kb_ref, vb_ref, kih_ref, kil_ref):
    h = _rms(x_ref[0], g_ref[...]).astype(BF)
    z = _dot(h, w_ref[...])
    a_ref[0] = z[:, E_A:E_G] * jax.nn.sigmoid(z[:, E_G:E_Q])
    c, sa, sb = c_ref[...], sa_ref[...], sb_ref[...]
    half = ROT_B // 2
    qk = z[:, E_Q:E_V]
    ssq = _dot2(qk * qk, ind_ref[...])
    r = lax.rsqrt(ssq / DH_B + EPS)
    qk = qk * _dot2(r, indt_ref[...]) * nrm_ref[...]
    for s in range(4):
        slab = _rope128(qk[:, s * LANES:(s + 1) * LANES], c, sa, sb, half)
        q_ref[0, :, s * LANES:(s + 1) * LANES] = (slab * (DH_B ** -0.5)).astype(BF)
    k = _rope128(qk[:, 4 * LANES:5 * LANES], c, sa, sb, half)
    k_ref[0] = k
    kb_ref[0] = k.astype(BF)
    v = z[:, E_V:E_QI]
    v_ref[0] = v
    vb_ref[0] = v.astype(BF)
    for s in range(2):
        lo = E_QI + s * LANES
        hi, lw = _split2(_rope128(z[:, lo:lo + LANES], c, sa, sb, half))
        qih_ref[0, :, s * LANES:(s + 1) * LANES] = hi
        qil_ref[0, :, s * LANES:(s + 1) * LANES] = lw
    ki = _rope128(z[:, E_KI:E_WI], c, sa, sb, half)
    ki_ref[0] = ki[:, :D_IDX]
    kih_ref[0], kil_ref[0] = _split2(ki)
    wi_ref[0] = z[:, E_WI:E_END] * IDX_W_SCALE


def _even_weights(w_in, q_norm, k_norm):
    cuts = [0, 2 * A_CH, 2 * A_CH + 512, 2 * A_CH + 640, 2 * A_CH + 768, 2 * A_CH + 1024, 2 * A_CH + 1088]
    glu = w_in[:, cuts[0]:cuts[1]]
    q = w_in[:, cuts[1]:cuts[2]]
    k = w_in[:, cuts[2]:cuts[3]]
    v = w_in[:, cuts[3]:cuts[4]]
    qi = w_in[:, cuts[4]:cuts[5]]
    ki = w_in[:, cuts[5]:cuts[6]]
    wi = w_in[:, cuts[6]:]
    q = q[:, _DSA_HEAD_PERM]
    pad = jnp.zeros((w_in.shape[0], LANES - H_IDX), w_in.dtype)
    w = jnp.concatenate([glu, q, k, v, qi, ki, ki, wi, pad], axis=1).astype(BF)
    nrm = jnp.concatenate([jnp.tile(q_norm, HB), jnp.tile(k_norm, HKV_B)])[None, :]
    return w, nrm


def _dsa_head_perm():
    j = np.arange(HB * DH_B)
    head = (j // LANES) + (HB // HKV_B) * ((j % LANES) // DH_B)
    return head * DH_B + (j % DH_B)


_DSA_HEAD_PERM = _dsa_head_perm()


def _even_proj(x, g, w, nrm, tables, tb):
    grp, t, _ = x.shape
    ind, indt = _indicator(5 * LANES, DH_B)
    blk = lambda width: pl.BlockSpec((1, tb, width), lambda i, j: (i, j, 0))
    tab = pl.BlockSpec((tb, LANES), lambda i, j: (j, 0))
    sds = lambda width, dt: jax.ShapeDtypeStruct((grp, t, width), dt)
    return pl.pallas_call(
        _even_proj_body,
        grid=(grp, t // tb),
        in_specs=[blk(D_MODEL), _full((1, D_MODEL)), _full(w.shape), _full(nrm.shape),
                  _full(ind.shape), _full(indt.shape), tab, tab, tab],
        out_specs=[blk(A_CH), blk(512), blk(LANES), blk(LANES), blk(D_IDX), blk(LANES),
                   blk(256), blk(256), blk(LANES), blk(LANES), blk(LANES), blk(LANES)],
        out_shape=[sds(A_CH, F32), sds(512, BF), sds(LANES, F32), sds(LANES, F32), sds(D_IDX, F32),
                   sds(LANES, F32), sds(256, BF), sds(256, BF), sds(LANES, BF), sds(LANES, BF),
                   sds(LANES, BF), sds(LANES, BF)],
        compiler_params=_params(("parallel", "parallel")),
        name="even_proj",
    )(x, g[None, :], w, nrm, ind, indt, *tables)


def _conv_tail_body(a_ref, prev_ref, w_ref, b_ref, g_ref, beta_ref, o_ref, buf, shifted, *, tb, pad, stride, rows, carry):
    j = pl.program_id(1)

    @pl.when(j == 0)
    def _():
        buf[0:pad] = prev_ref[0]

    buf[pad:pad + tb] = a_ref[0]
    base = pad - (CONV_A_WIDTH - 1) * stride
    offsets = [base + tap * stride for tap in range(CONV_A_WIDTH)]
    span = pad + tb - SUBLANES
    for r in sorted({o % SUBLANES for o in offsets} - {0}):
        shifted[r - 1, 0:span] = buf[pl.ds(r, span), :]
    for r0 in range(0, tb, rows):
        acc = jnp.broadcast_to(b_ref[...], (rows, A_CH))
        for tap, o in enumerate(offsets):
            r = o % SUBLANES
            src = buf[pl.ds(o + r0, rows), :] if r == 0 else shifted[r - 1, pl.ds(o - r + r0, rows), :]
            acc = acc + w_ref[tap:tap + 1, :] * src
        mu = jnp.mean(acc, axis=-1, keepdims=True)
        d = acc - mu
        var = jnp.mean(d * d, axis=-1, keepdims=True)
        y = d * lax.rsqrt(var + EPS) * g_ref[...] + beta_ref[...]
        o_ref[0, r0:r0 + rows, :] = (y * jax.nn.sigmoid(y)).astype(BF)
    if carry:
        buf[0:pad] = buf[tb:tb + pad]


def _conv_tail(a, prev, w, b, g, beta, tb, stride, rows):
    grp, t, _ = a.shape
    pad = prev.shape[1]
    wp = jnp.concatenate([w, jnp.zeros((1, A_CH), w.dtype)], axis=0)
    n_shift = SUBLANES - 1 if stride % SUBLANES else 1
    body = functools.partial(_conv_tail_body, tb=tb, pad=pad, stride=stride, rows=rows, carry=t > tb)
    return pl.pallas_call(
        body,
        grid=(grp, t // tb),
        in_specs=[pl.BlockSpec((1, tb, A_CH), lambda i, j: (i, j, 0)),
                  pl.BlockSpec((1, pad, A_CH), lambda i, j: (i, 0, 0)),
                  _full(wp.shape), _full((1, A_CH)), _full((1, A_CH)), _full((1, A_CH))],
        out_specs=pl.BlockSpec((1, tb, A_CH), lambda i, j: (i, j, 0)),
        out_shape=jax.ShapeDtypeStruct((grp, t, A_CH), BF),
        scratch_shapes=[pltpu.VMEM((pad + tb, A_CH), F32),
                        pltpu.VMEM((n_shift, pad + tb, A_CH), F32)],
        compiler_params=_params(("parallel", "arbitrary")),
        name="conv_tail",
    )(a, prev, wp, b[None, :], g[None, :], beta[None, :])


def _sort_keys(score):
    score = jnp.where(score == 0.0, 0.0, score)
    bits = pltpu.bitcast(score, jnp.int32)
    return jnp.where(bits < 0, bits ^ jnp.int32(0x7FFFFFFF), bits)


def _kth_largest(key_ref, width, nsel):
    rows = key_ref.shape[0]
    int_min = jnp.int32(-2 ** 31)

    def step(it, thr):
        cand = thr + lax.shift_left(jnp.int32(1), 31 - it)
        cnt = jnp.sum(jnp.where(key_ref[:, 0:width] >= cand, 1.0, 0.0), axis=-1, keepdims=True)
        return jnp.where(cnt >= nsel, cand, thr)

    return lax.fori_loop(0, 32, step, jnp.full((rows, 1), int_min, jnp.int32), unroll=2)


def _select_bias(key_ref, bias_ref, valid, tri_ref, width, nsel, chunk, row_ok=None):
    thr = _kth_largest(key_ref, width, nsel)
    key = key_ref[:, 0:width]
    cnt_gt = jnp.sum(jnp.where(key > thr, 1.0, 0.0), axis=-1, keepdims=True)
    cnt_eq = jnp.sum(jnp.where(key == thr, 1.0, 0.0), axis=-1, keepdims=True)
    need = nsel - cnt_gt
    bias_ref[:, 0:width] = jnp.where(jnp.logical_and(key >= thr, valid), 0.0, NEG)

    surplus = cnt_eq - need
    if row_ok is not None:
        surplus = jnp.where(row_ok, surplus, 0.0)

    @pl.when(jnp.max(surplus) > 0.0)
    def _():
        off = jnp.zeros_like(need)
        for c0 in range(0, width, chunk):
            kc = key_ref[:, c0:c0 + chunk]
            eq = kc == thr
            eqf = jnp.where(eq, 1.0, 0.0)
            rank = _dot(eqf.astype(BF), tri_ref[...]) + off
            take = jnp.logical_or(kc > thr, jnp.logical_and(eq, rank < need))
            bias_ref[:, c0:c0 + chunk] = jnp.where(jnp.logical_and(take, valid[:, c0:c0 + chunk]), 0.0, NEG)
            off = off + jnp.sum(eqf, axis=-1, keepdims=True)


def _strict_upper(n):
    return (jnp.arange(n)[:, None] < jnp.arange(n)[None, :]).astype(BF)


def _stack_heads(x2, low, keep_low):
    zero = jnp.zeros_like(x2)
    return jnp.where(low, x2, zero) if keep_low else jnp.where(low, zero, x2)


DSA_STACK_ROWS = 512


def _dsa_prompt_block(q_ref, qi_ref, wi_ref, k_ref, v_ref, ki_ref, tri_ref, o_ref, key_ref, bias_ref,
                      *, tq, sx, nsel):
    i = pl.program_id(1)
    low = _low_half(tq)
    qpos = i * tq + lax.broadcasted_iota(jnp.int32, (tq, sx), 0)
    kpos = lax.broadcasted_iota(jnp.int32, (tq, sx), 1)
    valid = kpos <= qpos
    per_dot = max(1, DSA_STACK_ROWS // tq)

    def stack(ref, heads, pick_low):
        return jnp.concatenate([_stack_heads(ref[0, :, (h // 2) * LANES:(h // 2 + 1) * LANES], low, pick_low(h))
                                for h in heads], axis=0)

    wi = wi_ref[0]
    score = jnp.zeros((tq, sx), F32)
    for h0 in range(0, H_IDX, per_dot):
        heads = range(h0, min(H_IDX, h0 + per_dot))
        logits = _dot_nt(stack(qi_ref, heads, lambda h: h % 2 == 0), ki_ref[0, 0:sx, :])
        for n, h in enumerate(heads):
            score = score + wi[:, h:h + 1] * jnp.maximum(logits[n * tq:(n + 1) * tq], 0.0)
    score = jnp.where(valid, score, -jnp.inf)
    key_ref[:, 0:sx] = _sort_keys(score)
    _select_bias(key_ref, bias_ref, valid, tri_ref, sx, nsel, LANES)

    bias = bias_ref[:, 0:sx]
    n_pairs = HB // HKV_B
    for p0 in range(0, n_pairs, per_dot):
        pairs = range(p0, min(n_pairs, p0 + per_dot))
        biasn = jnp.concatenate([bias] * len(pairs), axis=0)
        outs = []
        for g in range(HKV_B):
            qs = stack(q_ref, [2 * p for p in pairs], lambda h: g == 0)
            s = _dot_nt(qs, k_ref[0, 0:sx, :]) + biasn
            e = jnp.exp(s - jnp.max(s, axis=-1, keepdims=True))
            l = jnp.sum(e, axis=-1, keepdims=True)
            outs.append(_dot(e.astype(BF), v_ref[0, 0:sx, :]) / l)
        for n, p in enumerate(pairs):
            o_ref[0, :, p * LANES:(p + 1) * LANES] = jnp.where(
                low, outs[0][n * tq:(n + 1) * tq], outs[1][n * tq:(n + 1) * tq]).astype(BF)


def _causal_extents(n_blocks, tq, s):
    nb = 4 if n_blocks % 4 == 0 else 1
    per = n_blocks // nb
    return per, [min(s, (c + 1) * per * tq) for c in range(nb)]


def _dsa_prompt_body(*refs, tq, s, nsel):
    per, extents = _causal_extents(s // tq, tq, s)
    i = pl.program_id(1)
    for c, sx in enumerate(extents):
        pl.when(i // per == c)(functools.partial(_dsa_prompt_block, *refs, tq=tq, sx=sx, nsel=nsel))


def _dsa_prompt(q, qi, wi, kb, vb, ki, tq):
    n, s, _ = q.shape
    nsel = min(TOPK_MAX, s // 4)
    tri = _strict_upper(LANES)
    qblk = lambda width: pl.BlockSpec((1, tq, width), lambda i, j: (i, j, 0))
    sblk = lambda width: pl.BlockSpec((1, s, width), lambda i, j: (i, 0, 0))
    return pl.pallas_call(
        functools.partial(_dsa_prompt_body, tq=tq, s=s, nsel=nsel),
        grid=(n, s // tq),
        in_specs=[qblk(512), qblk(256), qblk(LANES), sblk(LANES), sblk(LANES), sblk(LANES), _full(tri.shape)],
        out_specs=qblk(512),
        out_shape=jax.ShapeDtypeStruct((n, s, 512), BF),
        scratch_shapes=[pltpu.VMEM((tq, s), jnp.int32), pltpu.VMEM((tq, s), F32)],
        compiler_params=_params(("parallel", "parallel")),
        name="dsa_prompt",
    )(q, qi, wi, kb, vb, ki, tri)


FFN_TILE = 256


def _ffn_body(x_ref, a_ref, b_ref, wa_ref, wb_ref, g_ref, wu_ref, cw_ref, cb_ref, wd_ref, prev_ref,
              o_ref, tail_ref, ybuf, hbuf, carry, ubuf, act, *, tb, pad, stride):
    j = pl.program_id(1)
    ybuf[...] = x_ref[0] + _dot(a_ref[0], wa_ref[...]) + _dot(b_ref[0], wb_ref[...])
    hbuf[...] = _rms(ybuf[...], g_ref[...]).astype(BF)

    @pl.when(j == 0)
    def _():
        carry[...] = prev_ref[0]

    for f in range(D_FF // FFN_TILE):
        halves = []
        for part in range(2):
            c0 = part * D_FF + f * FFN_TILE
            u = _dot(hbuf[...], wu_ref[:, c0:c0 + FFN_TILE])
            ubuf[0:pad] = carry[:, c0:c0 + FFN_TILE]
            ubuf[pad:pad + tb] = u
            y = (cw_ref[0:1, c0:c0 + FFN_TILE] * ubuf[pl.ds(pad - 2 * stride, tb), :]
                 + cw_ref[1:2, c0:c0 + FFN_TILE] * ubuf[pl.ds(pad - stride, tb), :]
                 + cw_ref[2:3, c0:c0 + FFN_TILE] * u + cb_ref[:, c0:c0 + FFN_TILE])
            carry[:, c0:c0 + FFN_TILE] = ubuf[tb:tb + pad]
            halves.append(y)
        act[:, f * FFN_TILE:(f + 1) * FFN_TILE] = (halves[1] * jax.nn.sigmoid(halves[1]) * halves[0]).astype(BF)
    o_ref[0] = ybuf[...] + _dot(act[...], wd_ref[...])

    @pl.when(j == pl.num_programs(1) - 1)
    def _():
        tail_ref[0] = carry[...]


def _ffn(x, a, b, wa, wb, g, wu, cw, cb, wd, prev, tb, stride):
    grp, t, _ = x.shape
    pad = prev.shape[1]
    cwp = jnp.concatenate([cw, jnp.zeros((SUBLANES - FFN_CONV_WIDTH, 2 * D_FF), cw.dtype)], axis=0)
    blk = lambda width: pl.BlockSpec((1, tb, width), lambda i, j: (i, j, 0))
    pblk = pl.BlockSpec((1, pad, 2 * D_FF), lambda i, j: (i, 0, 0))
    once = lambda shape: pl.BlockSpec(shape, lambda i, j: (0,) * len(shape), pipeline_mode=pl.Buffered(1))
    return pl.pallas_call(
        functools.partial(_ffn_body, tb=tb, pad=pad, stride=stride),
        grid=(grp, t // tb),
        in_specs=[blk(D_MODEL), blk(a.shape[-1]), blk(b.shape[-1]), once(wa.shape), once(wb.shape),
                  _full((1, D_MODEL)), once(wu.shape), _full(cwp.shape), _full((1, 2 * D_FF)),
                  once(wd.shape), pblk],
        out_specs=[blk(D_MODEL), pblk],
        out_shape=[jax.ShapeDtypeStruct(x.shape, F32), jax.ShapeDtypeStruct(prev.shape, F32)],
        scratch_shapes=[pltpu.VMEM((tb, D_MODEL), F32), pltpu.VMEM((tb, D_MODEL), BF),
                        pltpu.VMEM((pad, 2 * D_FF), F32), pltpu.VMEM((pad + tb, FFN_TILE), F32),
                        pltpu.VMEM((tb, D_FF), BF)],
        compiler_params=_params(("parallel", "arbitrary")),
        name="conv_ffn",
    )(x, a, b, wa, wb, g[None, :], wu, cwp, cb[None, :], wd, prev)


def _head_norm128(x, gain, ind_ref, indt_ref, dim):
    ssq = _dot2(x * x, ind_ref[...])
    r = lax.rsqrt(ssq / dim + EPS)
    return x * _dot2(r, indt_ref[...]) * gain


def _odd_proj_body(x_ref, g_ref, w_ref, qan_ref, wqb_ref, kvn_ref, qn_ref, kn_ref, ind_ref, indt_ref,
                   c_ref, sa_ref, sb_ref, glg_ref, glb_ref, wuk_ref, wuv_ref,
                   ckv_ref, kpe_ref, u_ref, v_ref, q_ref, k_ref, vv_ref):
    h = _rms(x_ref[0], g_ref[...]).astype(BF)
    z = _dot(h, w_ref[...])
    c, sa, sb = c_ref[...], sa_ref[...], sb_ref[...]
    half = ROPE_C // 2

    qa = _rms(z[:, O_QA:O_CKV], qan_ref[...]).astype(BF)
    q = _head_norm128(_dot(qa, wqb_ref[...]), qn_ref[...], ind_ref, indt_ref, DQK_C)
    for hd in range(HC):
        sl = slice(hd * LANES, (hd + 1) * LANES)
        q_ref[0, :, sl] = (_rope128(q[:, sl], c, sa, sb, half) * MLA_SCALE).astype(BF)

    ckv = _rms(z[:, O_CKV:O_KPE], kvn_ref[...])
    ckv_ref[0] = ckv
    cb = ckv.astype(BF)
    kpe = z[:, O_KPE:O_U]
    kpe_ref[0] = kpe[:, :ROPE_C]
    kfull = _dot(cb, wuk_ref[...]) + jnp.concatenate([kpe] * HC, axis=1)
    k = _head_norm128(kfull, kn_ref[...], ind_ref, indt_ref, DQK_C)
    for hd in range(HC):
        sl = slice(hd * LANES, (hd + 1) * LANES)
        k_ref[0, :, sl] = _rope128(k[:, sl], c, sa, sb, half).astype(BF)
    vv_ref[0] = _dot(cb, wuv_ref[...]).astype(BF)

    zz = jax.nn.gelu(z[:, O_U:O_END])
    u_ref[0] = zz[:, :D_CH]
    vz = zz[:, D_CH:]
    mu = jnp.mean(vz, axis=-1, keepdims=True)
    d = vz - mu
    var = jnp.mean(d * d, axis=-1, keepdims=True)
    v_ref[0] = d * lax.rsqrt(var + EPS) * glg_ref[...] + glb_ref[...]


def _pad_heads(w, lead):
    z32 = jnp.zeros(lead + (HC, ROPE_C), w.dtype)
    return jnp.concatenate([z32, w, z32], axis=-1).reshape(lead + (HC * LANES,))


def _odd_weights(w_in, w_qb, w_uk, w_uv, q_norm, k_norm):
    d = w_in.shape[0]
    cuts = [Q_LORA, Q_LORA + KV_LORA, Q_LORA + KV_LORA + ROPE_C]
    pad = jnp.zeros((d, LANES - ROPE_C), w_in.dtype)
    w = jnp.concatenate([w_in[:, :cuts[2]], pad, w_in[:, cuts[2]:]], axis=1).astype(BF)
    qb = w_qb.reshape(Q_LORA, HC, DQK_C)
    qb = jnp.concatenate([qb, jnp.zeros((Q_LORA, HC, LANES - DQK_C), w_qb.dtype)], axis=-1)
    wqb = qb.reshape(Q_LORA, HC * LANES).astype(BF)
    wuk_pad = _pad_heads(w_uk, (KV_LORA,)).astype(BF)
    wuk = w_uk.reshape(KV_LORA, HC * NOPE_C).astype(BF)
    wuv = w_uv.reshape(KV_LORA, HC * VD_C).astype(BF)
    gain = lambda g: jnp.tile(jnp.concatenate([g, jnp.zeros((LANES - DQK_C,), g.dtype)]), HC)[None, :]
    return w, wqb, wuk_pad, wuk, wuv, gain(q_norm), gain(k_norm)


def _odd_proj(x, g, w, wqb, wuk_pad, wuv, qan, kvn, qn, kn, glg, glb, tables, tb):
    grp, t, _ = x.shape
    ind, indt = _indicator(HC * LANES, LANES)
    blk = lambda width: pl.BlockSpec((1, tb, width), lambda i, j: (i, j, 0))
    tab = pl.BlockSpec((tb, LANES), lambda i, j: (j, 0))
    sds = lambda width, dt: jax.ShapeDtypeStruct((grp, t, width), dt)
    row = lambda v: v[None, :]
    return pl.pallas_call(
        _odd_proj_body,
        grid=(grp, t // tb),
        in_specs=[blk(D_MODEL), _full((1, D_MODEL)), _full(w.shape), _full((1, Q_LORA)), _full(wqb.shape),
                  _full((1, KV_LORA)), _full(qn.shape), _full(kn.shape), _full(ind.shape), _full(indt.shape),
                  tab, tab, tab, _full((1, D_CH)), _full((1, D_CH)), _full(wuk_pad.shape), _full(wuv.shape)],
        out_specs=[blk(KV_LORA), blk(ROPE_C), blk(D_CH), blk(D_CH), blk(HC * LANES), blk(HC * LANES),
                   blk(HC * VD_C)],
        out_shape=[sds(KV_LORA, F32), sds(ROPE_C, F32), sds(D_CH, F32), sds(D_CH, F32),
                   sds(HC * LANES, BF), sds(HC * LANES, BF), sds(HC * VD_C, BF)],
        compiler_params=_params(("parallel", "parallel")),
        name="odd_proj",
    )(x, row(g), w, row(qan), wqb, row(kvn), qn, kn, ind, indt, *tables, row(glg), row(glb), wuk_pad, wuv)


def _gmlp_prompt_body(u_ref, v_ref, w_ref, b_ref, o_ref, *, tb):
    rows = D_GROUPS * CHUNK
    t_in = lax.broadcasted_iota(jnp.int32, (rows, CHUNK), 0) % CHUNK
    s_in = lax.broadcasted_iota(jnp.int32, (rows, CHUNK), 1)
    w = jnp.where(s_in <= t_in, w_ref[...], 0.0).astype(BF)
    grp = lax.broadcasted_iota(jnp.int32, (CHUNK, D_CH), 1) // (D_CH // D_GROUPS)
    for c0 in range(0, tb, CHUNK):
        y = _dot(w, v_ref[0, c0:c0 + CHUNK, :].astype(BF))
        mix = b_ref[...]
        for gi in range(D_GROUPS):
            mix = mix + jnp.where(grp == gi, y[gi * CHUNK:(gi + 1) * CHUNK], 0.0)
        o_ref[0, c0:c0 + CHUNK, :] = (u_ref[0, c0:c0 + CHUNK, :] * mix).astype(BF)


def _gmlp_prompt(u, v, w_s, b_s, tb):
    n, s, _ = u.shape
    w = w_s.reshape(D_GROUPS * CHUNK, CHUNK)
    b = jnp.repeat(b_s.T, D_CH // D_GROUPS, axis=1)
    blk = pl.BlockSpec((1, tb, D_CH), lambda i, j: (i, j, 0))
    return pl.pallas_call(
        functools.partial(_gmlp_prompt_body, tb=tb),
        grid=(n, s // tb),
        in_specs=[blk, blk, _full(w.shape), _full(b.shape)],
        out_specs=blk,
        out_shape=jax.ShapeDtypeStruct(u.shape, BF),
        compiler_params=_params(("parallel", "parallel")),
        name="gmlp_prompt",
    )(u, v, w, b)


def _gmlp_sample_body(u_ref, v_ref, w_ref, b_ref, o_ref, *, nb, nt):
    for t in range(nt):
        mix = jnp.broadcast_to(b_ref[t:t + 1, :], (nb, D_CH))
        for s in range(t + 1):
            mix = mix + w_ref[t * nt + s:t * nt + s + 1, :] * v_ref[s * nb:(s + 1) * nb, :]
        o_ref[t * nb:(t + 1) * nb, :] = (u_ref[t * nb:(t + 1) * nb, :] * mix).astype(BF)


def _gmlp_sample(u, v, w_s, b_s, nb, nt):
    lane = D_CH // D_GROUPS
    w = jnp.repeat(jnp.transpose(w_s[:, :nt, :nt], (1, 2, 0)).reshape(nt * nt, D_GROUPS), lane, axis=1)
    b = jnp.repeat(b_s.T[:nt], lane, axis=1)
    return pl.pallas_call(
        functools.partial(_gmlp_sample_body, nb=nb, nt=nt),
        out_shape=jax.ShapeDtypeStruct(u.shape, BF),
        name="gmlp_sample",
    )(u, v, w, b)


def _mla_prompt_block(q_ref, k_ref, v_ref, o_ref, *, tq, sx):
    i = pl.program_id(1)
    low = _low_half(tq)
    qpos = i * tq + lax.broadcasted_iota(jnp.int32, (tq, sx), 0)
    kpos = lax.broadcasted_iota(jnp.int32, (tq, sx), 1)
    bias = jnp.where(kpos <= qpos, 0.0, NEG)
    for p in range(HC // 2):
        outs = []
        for hd in (2 * p, 2 * p + 1):
            sl = slice(hd * LANES, (hd + 1) * LANES)
            s = _dot_nt(q_ref[0, :, sl], k_ref[0, 0:sx, sl]) + bias
            e = jnp.exp(s - jnp.max(s, axis=-1, keepdims=True))
            l = jnp.sum(e, axis=-1, keepdims=True)
            outs.append(_dot(e.astype(BF), v_ref[0, 0:sx, p * LANES:(p + 1) * LANES]) / l)
        o_ref[0, :, p * LANES:(p + 1) * LANES] = jnp.where(low, outs[0], outs[1]).astype(BF)


def _mla_prompt_body(*refs, tq, s):
    per, extents = _causal_extents(s // tq, tq, s)
    i = pl.program_id(1)
    for c, sx in enumerate(extents):
        pl.when(i // per == c)(functools.partial(_mla_prompt_block, *refs, tq=tq, sx=sx))


def _mla_prompt(q, k, v, tq):
    n, s, _ = q.shape
    return pl.pallas_call(
        functools.partial(_mla_prompt_body, tq=tq, s=s),
        grid=(n, s // tq),
        in_specs=[pl.BlockSpec((1, tq, HC * LANES), lambda i, j: (i, j, 0)),
                  pl.BlockSpec((1, s, HC * LANES), lambda i, j: (i, 0, 0)),
                  pl.BlockSpec((1, s, HC * VD_C), lambda i, j: (i, 0, 0))],
        out_specs=pl.BlockSpec((1, tq, HC * VD_C), lambda i, j: (i, j, 0)),
        out_shape=jax.ShapeDtypeStruct((n, s, HC * VD_C), BF),
        compiler_params=_params(("parallel", "parallel")),
        name="mla_prompt",
    )(q, k, v)


PAGES_PER_STEP = 32


def _page_specs(cache_shape, layer):
    _, _, rows, width = cache_shape

    def spec(slot):
        return pl.BlockSpec((None, None, rows, width),
                            lambda s, j, pt: (layer, pt[s, j * PAGES_PER_STEP + slot], 0, 0))

    return [spec(slot) for slot in range(PAGES_PER_STEP)]


def _cat_pages(refs, axis):
    return jnp.concatenate([r[...] for r in refs], axis=axis)


def _idx_scores(qh, ql, wi, kh, kl):
    both = _dot(jnp.concatenate([qh, ql], axis=0), kh)
    rows = qh.shape[0]
    logits = both[0:rows] + both[rows:2 * rows] + _dot(qh, kl)
    score = jnp.zeros((SUBLANES, logits.shape[1]), F32)
    for h in range(H_IDX):
        sl = slice(h * SUBLANES, (h + 1) * SUBLANES)
        score = score + wi[sl, 0:1] * jnp.maximum(logits[sl], 0.0)
    return score


def _dsa_sample_score_body(pt_ref, qh_ref, ql_ref, wi_ref, *rest):
    del pt_ref
    pages, o_ref = rest[:PAGES_PER_STEP], rest[PAGES_PER_STEP]
    kh, kl = _split2(_cat_pages(pages, 1))
    o_ref[0] = _idx_scores(qh_ref[0], ql_ref[0], wi_ref[0], kh, kl)


def _dsa_sample_scores(page_table, qh, ql, wi, cache_ki, layer):
    n, n_pages = page_table.shape
    steps = n_pages // PAGES_PER_STEP
    span = PAGES_PER_STEP * PAGE_SIZE
    per_seq = lambda shape: pl.BlockSpec((1,) + shape, lambda s, j, pt: (s, 0, 0))
    grid_spec = pltpu.PrefetchScalarGridSpec(
        num_scalar_prefetch=1,
        grid=(n, steps),
        in_specs=[per_seq((4 * SUBLANES, D_IDX)), per_seq((4 * SUBLANES, D_IDX)), per_seq((4 * SUBLANES, LANES))]
        + _page_specs(cache_ki.shape, layer),
        out_specs=pl.BlockSpec((1, SUBLANES, span), lambda s, j, pt: (s, 0, j)),
    )
    return pl.pallas_call(
        _dsa_sample_score_body,
        grid_spec=grid_spec,
        out_shape=jax.ShapeDtypeStruct((n, SUBLANES, n_pages * PAGE_SIZE), F32),
        compiler_params=_params(("parallel", "arbitrary")),
        name="dsa_sample_scores",
    )(page_table, qh, ql, wi, *([cache_ki] * PAGES_PER_STEP))


SELECT_SEQS = 8


def _dsa_sample_select_body(score_ref, qh_ref, ql_ref, wi_ref, knh_ref, knl_ref, tri_ref, bias_ref, key_ref,
                            *, past, nsel, nt):
    width = past + PAGE_SIZE
    rows = SELECT_SEQS * SUBLANES
    slot = lax.broadcasted_iota(jnp.int32, (rows, width), 0) % SUBLANES
    kpos = lax.broadcasted_iota(jnp.int32, (rows, width), 1)
    valid = kpos <= past + slot
    for g in range(SELECT_SEQS):
        sl = slice(g * SUBLANES, (g + 1) * SUBLANES)
        new = _idx_scores(qh_ref[g], ql_ref[g], wi_ref[g], knh_ref[g], knl_ref[g])
        key_ref[sl, 0:past] = _sort_keys(score_ref[g])
        key_ref[sl, past:width] = _sort_keys(jnp.where(valid[sl, past:width], new, -jnp.inf))
    row_ok = lax.broadcasted_iota(jnp.int32, (rows, 1), 0) % SUBLANES < nt
    _select_bias(key_ref, bias_ref, valid, tri_ref, width, nsel, LANES, row_ok)


def _dsa_sample_select(score, qh, ql, wi, knh, knl, nt):
    n, _, past = score.shape
    nsel = min(TOPK_MAX, (past + nt) // 4)
    tri = _strict_upper(LANES)
    width = past + PAGE_SIZE
    rows = SELECT_SEQS * SUBLANES
    blk = lambda a: pl.BlockSpec((SELECT_SEQS,) + a.shape[1:], lambda i: (i, 0, 0))
    return pl.pallas_call(
        functools.partial(_dsa_sample_select_body, past=past, nsel=nsel, nt=nt),
        grid=(n // SELECT_SEQS,),
        in_specs=[blk(score), blk(qh), blk(ql), blk(wi), blk(knh), blk(knl), _full(tri.shape)],
        out_specs=pl.BlockSpec((rows, width), lambda i: (i, 0)),
        out_shape=jax.ShapeDtypeStruct((n * SUBLANES, width), F32),
        scratch_shapes=[pltpu.VMEM((rows, width), jnp.int32)],
        compiler_params=_params(("parallel",)),
        name="dsa_sample_select",
    )(score, qh, ql, wi, knh, knl, tri)


def _dsa_sample_attend_body(pt_ref, bias_ref, btail_ref, q_ref, kn_ref, vn_ref, *rest):
    del pt_ref
    kpages = rest[:PAGES_PER_STEP]
    vpages = rest[PAGES_PER_STEP:2 * PAGES_PER_STEP]
    o_ref, q64_ref, m_ref, l_ref, acc_ref = rest[2 * PAGES_PER_STEP:]
    j = pl.program_id(1)
    rows8 = SUBLANES
    per = HB // HKV_B

    @pl.when(j == 0)
    def _():
        low = _low_half(rows8)
        q8 = q_ref[0].astype(F32)
        q64_ref[...] = jnp.concatenate(
            [_stack_heads(q8[:, (h % per) * LANES:(h % per + 1) * LANES], low, h < per) for h in range(HB)],
            axis=0).astype(BF)
        m_ref[...] = jnp.full(m_ref.shape, NEG, F32)
        l_ref[...] = jnp.zeros(l_ref.shape, F32)
        acc_ref[...] = jnp.zeros(acc_ref.shape, F32)

    def fold(bias8, k_t, v_t):
        s = _dot(q64_ref[...], k_t) + jnp.concatenate([bias8] * HB, axis=0)
        m_old = m_ref[...]
        m_new = jnp.maximum(m_old, jnp.max(s, axis=-1, keepdims=True))
        alpha = jnp.exp(m_old - m_new)
        e = jnp.exp(s - m_new)
        l_ref[...] = alpha * l_ref[...] + jnp.sum(e, axis=-1, keepdims=True)
        acc_ref[...] = alpha * acc_ref[...] + _dot_nt(e.astype(BF), v_t)
        m_ref[...] = m_new

    fold(bias_ref[0], _cat_pages(kpages, 1).astype(BF), _cat_pages(vpages, 1).astype(BF))

    @pl.when(j == pl.num_programs(1) - 1)
    def _():
        fold(btail_ref[0], kn_ref[0], vn_ref[0])
        out = acc_ref[...] / l_ref[...]
        low = _low_half(rows8)
        for p in range(per):
            o_ref[0, :, p * LANES:(p + 1) * LANES] = jnp.where(
                low, out[p * rows8:(p + 1) * rows8], out[(per + p) * rows8:(per + p + 1) * rows8]).astype(BF)


def _dsa_sample_attend(page_table, bias, q8, kn, vn, cache_k, cache_v, layer):
    n, n_pages = page_table.shape
    span = PAGES_PER_STEP * PAGE_SIZE
    per_seq = lambda shape: pl.BlockSpec((1,) + shape, lambda s, j, pt: (s, 0, 0))
    grid_spec = pltpu.PrefetchScalarGridSpec(
        num_scalar_prefetch=1,
        grid=(n, n_pages // PAGES_PER_STEP),
        in_specs=[pl.BlockSpec((1, SUBLANES, span), lambda s, j, pt: (s, 0, j)),
                  pl.BlockSpec((1, SUBLANES, PAGE_SIZE), lambda s, j, pt: (s, 0, n_pages)),
                  per_seq((SUBLANES, HB * DH_B)), per_seq((LANES, PAGE_SIZE)), per_seq((LANES, PAGE_SIZE))]
        + _page_specs(cache_k.shape, layer) + _page_specs(cache_v.shape, layer),
        out_specs=per_seq((SUBLANES, HB * DH_B)),
        scratch_shapes=[pltpu.VMEM((HB * SUBLANES, LANES), BF), pltpu.VMEM((HB * SUBLANES, 1), F32),
                        pltpu.VMEM((HB * SUBLANES, 1), F32), pltpu.VMEM((HB * SUBLANES, LANES), F32)],
    )
    return pl.pallas_call(
        _dsa_sample_attend_body,
        grid_spec=grid_spec,
        out_shape=jax.ShapeDtypeStruct((n, SUBLANES, HB * DH_B), BF),
        compiler_params=_params(("parallel", "arbitrary")),
        name="dsa_sample_attend",
    )(page_table, bias, bias, q8, kn, vn, *([cache_k] * PAGES_PER_STEP), *([cache_v] * PAGES_PER_STEP))


MLA_ROWS = HC * SUBLANES


def _mla_sample_body(pt_ref, q_ref, kn_ref, vn_ref, wukp_ref, wukt_ref, wuv_ref, gk_ref, ct_ref, st_ref,
                     *rest, nt):
    del pt_ref
    cpages = rest[:PAGES_PER_STEP]
    ppages = rest[PAGES_PER_STEP:2 * PAGES_PER_STEP]
    o_ref, qbd_ref, lhs_ref, qrot_ref, m_ref, l_ref, acc_ref = rest[2 * PAGES_PER_STEP:]
    j = pl.program_id(1)
    rows = MLA_ROWS
    feat = HC * NOPE_C

    @pl.when(j == 0)
    def _():
        q8 = q_ref[0].astype(F32)
        rhead = lax.broadcasted_iota(jnp.int32, (rows, HC * LANES), 0) // SUBLANES
        lhead = lax.broadcasted_iota(jnp.int32, (rows, HC * LANES), 1) // LANES
        qbd = jnp.where(rhead == lhead, jnp.concatenate([q8] * HC, axis=0), 0.0)
        qbd_ref[...] = qbd.astype(BF)
        lhs_ref[0:feat, :] = wukt_ref[...]
        lhs_ref[feat:feat + rows, :] = _dot_nt((qbd * gk_ref[...]).astype(BF), wukp_ref[...]).astype(BF)
        folded = qbd[:, 0:LANES]
        for hd in range(1, HC):
            folded = folded + qbd[:, hd * LANES:(hd + 1) * LANES]
        lane = lax.broadcasted_iota(jnp.int32, (rows, LANES), 1)
        half = ROPE_C // 2
        swapped = jnp.where(lane < half, pltpu.roll(folded, LANES - half, 1),
                            jnp.where(lane < ROPE_C, -pltpu.roll(folded, half, 1), 0.0))
        gpe = gk_ref[:, 0:LANES]
        qrot_ref[...] = jnp.concatenate([(folded * gpe)[:, 0:ROPE_C], (swapped * gpe)[:, 0:ROPE_C]],
                                        axis=1).astype(BF)
        m_ref[...] = jnp.full(m_ref.shape, NEG, F32)
        l_ref[...] = jnp.zeros(l_ref.shape, F32)
        acc_ref[...] = jnp.zeros(acc_ref.shape, F32)

    cb = _cat_pages(cpages, 0).astype(BF)
    kpe = _cat_pages(ppages, 1)
    big = _dot_nt(lhs_ref[...], cb)
    kp2 = jnp.sum(kpe * kpe, axis=0, keepdims=True)
    rot = _dot(qrot_ref[...], jnp.concatenate([kpe * ct_ref[...], kpe * st_ref[...]], axis=0).astype(BF))
    parts = []
    for hd in range(HC):
        kn = big[hd * NOPE_C:(hd + 1) * NOPE_C]
        r = lax.rsqrt((jnp.sum(kn * kn, axis=0, keepdims=True) + kp2) / DQK_C + EPS)
        sl = slice(hd * SUBLANES, (hd + 1) * SUBLANES)
        parts.append(r * (big[feat + hd * SUBLANES:feat + (hd + 1) * SUBLANES] + rot[sl]))
    s = jnp.concatenate(parts, axis=0)

    m_old = m_ref[...]
    m_new = jnp.maximum(m_old, jnp.max(s, axis=-1, keepdims=True))
    alpha = jnp.exp(m_old - m_new)
    e = jnp.exp(s - m_new)
    l_ref[...] = alpha * l_ref[...] + jnp.sum(e, axis=-1, keepdims=True)
    acc_ref[...] = alpha * acc_ref[...] + _dot(e.astype(BF), cb)
    m_ref[...] = m_new

    @pl.when(j == pl.num_programs(1) - 1)
    def _():
        s_new = _dot_nt(qbd_ref[...], kn_ref[0])
        t_row = lax.broadcasted_iota(jnp.int32, s_new.shape, 0) % SUBLANES
        col = lax.broadcasted_iota(jnp.int32, s_new.shape, 1)
        s_new = jnp.where(jnp.logical_and(col <= t_row, col < nt), s_new, NEG)
        m_old = m_ref[...]
        m_fin = jnp.maximum(m_old, jnp.max(s_new, axis=-1, keepdims=True))
        alpha = jnp.exp(m_old - m_fin)
        e = jnp.exp(s_new - m_fin)
        l = alpha * l_ref[...] + jnp.sum(e, axis=-1, keepdims=True)
        out = (_dot((alpha * acc_ref[...]).astype(BF), wuv_ref[...]) + _dot(e.astype(BF), vn_ref[0])) / l
        lhead = lax.broadcasted_iota(jnp.int32, (SUBLANES, HC * VD_C), 1) // VD_C
        res = jnp.zeros((SUBLANES, HC * VD_C), F32)
        for hd in range(HC):
            res = res + jnp.where(lhead == hd, out[hd * SUBLANES:(hd + 1) * SUBLANES], 0.0)
        o_ref[0] = res.astype(BF)


def _mla_sample(page_table, q8, kn, vn, wuk_pad, wuk, wuv, gk, ctab, stab, cache_ckv, cache_kpe, layer, nt):
    n, n_pages = page_table.shape
    span = PAGES_PER_STEP * PAGE_SIZE
    rows = MLA_ROWS
    wukt = wuk.T
    per_seq = lambda shape: pl.BlockSpec((1,) + shape, lambda s, j, pt: (s, 0, 0))
    const = lambda a: pl.BlockSpec(a.shape, lambda s, j, pt: (0,) * a.ndim)
    tab = pl.BlockSpec((ROPE_C, span), lambda s, j, pt: (0, j))
    grid_spec = pltpu.PrefetchScalarGridSpec(
        num_scalar_prefetch=1,
        grid=(n, n_pages // PAGES_PER_STEP),
        in_specs=[per_seq((SUBLANES, HC * LANES)), per_seq((PAGE_SIZE, HC * LANES)), per_seq((PAGE_SIZE, HC * VD_C)),
                  const(wuk_pad), const(wukt), const(wuv), const(gk), tab, tab]
        + _page_specs(cache_ckv.shape, layer) + _page_specs(cache_kpe.shape, layer),
        out_specs=per_seq((SUBLANES, HC * VD_C)),
        scratch_shapes=[pltpu.VMEM((rows, HC * LANES), BF), pltpu.VMEM((HC * NOPE_C + rows, KV_LORA), BF),
                        pltpu.VMEM((rows, 2 * ROPE_C), BF),
                        pltpu.VMEM((rows, 1), F32), pltpu.VMEM((rows, 1), F32), pltpu.VMEM((rows, KV_LORA), F32)],
    )
    return pl.pallas_call(
        functools.partial(_mla_sample_body, nt=nt),
        grid_spec=grid_spec,
        out_shape=jax.ShapeDtypeStruct((n, SUBLANES, HC * VD_C), BF),
        compiler_params=_params(("parallel", "arbitrary")),
        name="mla_sample",
    )(page_table, q8, kn, vn, wuk_pad, wukt, wuv, gk, ctab, stab,
      *([cache_ckv] * PAGES_PER_STEP), *([cache_kpe] * PAGES_PER_STEP))


PROMPT_ROWS = 512
CONV_ROWS = 256
DSA_QBLOCK = 256
MLA_QBLOCK = 512


def kernel(x_prompt, x_sample, cache_dsa_k, cache_dsa_v, cache_dsa_kidx, state_conv_a, cache_mla_ckv, cache_mla_kpe, state_ffn_conv, page_table, norm_mix, norm_ffn, w_in_e, conv_a_w, conv_a_b, conv_a_ln_g, conv_a_ln_b, q_norm_b, k_norm_b, w_out_e, w_in_o, q_a_norm, w_qb, kv_a_norm, w_uk, w_uv, q_norm_c, k_norm_c, gmlp_ln_g, gmlp_ln_b, w_spatial, b_spatial, w_out_o, w_up, ffn_conv_w, ffn_conv_b, w_down):
    bsz, seq, _ = x_prompt.shape
    nb, nt, _ = x_sample.shape
    depth = norm_mix.shape[0]
    past = page_table.shape[1] * PAGE_SIZE
    srows = nt * nb
    pos_p = jnp.arange(seq)
    pos_s = past + jnp.repeat(jnp.arange(nt), nb)

    def to_rows(a):
        return jnp.transpose(a, (1, 0, 2)).reshape(1, a.shape[1] * nb, a.shape[2])

    def to_seq(a):
        return jnp.transpose(a.reshape(-1, nb, a.shape[-1]), (1, 0, 2))

    def pad_rows(a, rows):
        return jnp.pad(a, ((0, 0), (0, rows - a.shape[1]), (0, 0)))

    def idx_stack(a):
        w = a.shape[-1] // H_IDX
        a = jnp.transpose(to_seq(a).reshape(nb, nt, H_IDX, w), (0, 2, 1, 3))
        a = jnp.pad(a, ((0, 0), (0, 0), (0, SUBLANES - nt), (0, 0)))
        return a.reshape(nb, H_IDX * SUBLANES, w)

    yp = x_prompt
    ys = to_rows(x_sample)
    pk, pv, pki, pca, pckv, pkpe, pff = [], [], [], [], [], [], []
    sk, sv, ski, sca, sckv, skpe, sgv, sff = [], [], [], [], [], [], [], []

    for layer in range(depth):
        if layer % 2 == 0:
            e = layer // 2
            w, nrm = _even_weights(w_in_e[e], q_norm_b[e], k_norm_b[e])
            tabs_p = _rope_tables(pos_p, ROT_B, DH_B)
            tabs_s = _rope_tables(pos_s, ROT_B, DH_B)
            ap, qp, kp, vp, kip, wip, qihp, qilp, kbp, vbp, kihp, kilp = _even_proj(
                yp, norm_mix[layer], w, nrm, tabs_p, PROMPT_ROWS)
            a_s, qs, ks, vs, kis, wis, qihs, qils, kbs, vbs, kihs, kils = _even_proj(
                ys, norm_mix[layer], w, nrm, tabs_s, srows)

            hist = CONV_A_WIDTH - 1
            conv = (conv_a_w[e], conv_a_b[e], conv_a_ln_g[e], conv_a_ln_b[e])
            cp = _conv_tail(ap, jnp.zeros((bsz, 32, A_CH), F32), *conv, CONV_ROWS, 1, 32)
            cs = _conv_tail(a_s, to_rows(state_conv_a[e]), *conv, srows, nb, nb)
            pca.append(ap[:, seq - hist:, :])
            sca.append(jnp.concatenate([state_conv_a[e], to_seq(a_s)], axis=1)[:, nt:, :])

            bp = _dsa_prompt(qp, qihp, wip, kbp, vbp, kihp, DSA_QBLOCK)

            qh32, ql32 = idx_stack(qihs), idx_stack(qils)
            wi32 = jnp.broadcast_to(idx_stack(wis[..., :H_IDX]), (nb, H_IDX * SUBLANES, LANES))
            n_pool = cache_dsa_k.shape[1]
            kv_t = lambda c: jnp.transpose(c, (0, 1, 3, 4, 2)).reshape(-1, n_pool, HKV_B * DH_B, PAGE_SIZE)
            new_t = lambda a: jnp.transpose(pad_rows(to_seq(a), PAGE_SIZE), (0, 2, 1))
            scores = _dsa_sample_scores(page_table, qh32, ql32, wi32, jnp.transpose(cache_dsa_kidx, (0, 1, 3, 2)), e)
            bias = _dsa_sample_select(scores, qh32, ql32, wi32, new_t(kihs[..., :D_IDX]), new_t(kils[..., :D_IDX]), nt)
            bs8 = _dsa_sample_attend(page_table, bias.reshape(nb, SUBLANES, -1), pad_rows(to_seq(qs), SUBLANES),
                                     new_t(kbs), new_t(vbs), kv_t(cache_dsa_k), kv_t(cache_dsa_v), e)
            bs = to_rows(bs8[:, :nt])

            wa = w_out_e[e][:A_CH].astype(BF)
            wb = w_out_e[e][A_CH:][_DSA_HEAD_PERM].astype(BF)
            mix_p, mix_s = (cp, bp), (cs, bs)

            pk.append(kp.reshape(bsz, seq, HKV_B, DH_B))
            pv.append(vp.reshape(bsz, seq, HKV_B, DH_B))
            pki.append(kip)
            sk.append(to_seq(ks).reshape(nb, nt, HKV_B, DH_B))
            sv.append(to_seq(vs).reshape(nb, nt, HKV_B, DH_B))
            ski.append(to_seq(kis))
        else:
            o = layer // 2
            w, wqb, wukp, wuk, wuv, qn, kn = _odd_weights(w_in_o[o], w_qb[o], w_uk[o], w_uv[o],
                                                          q_norm_c[o], k_norm_c[o])
            tabs_p = _rope_tables(pos_p, ROPE_C, LANES)
            tabs_s = _rope_tables(pos_s, ROPE_C, LANES)
            rest = (w, wqb, wukp, wuv, q_a_norm[o], kv_a_norm[o], qn, kn, gmlp_ln_g[o], gmlp_ln_b[o])
            ckvp, kpep, up, vp, q_p, k_p, v_p = _odd_proj(yp, norm_mix[layer], *rest, tabs_p, PROMPT_ROWS)
            ckvs, kpes, us, vs, q_s, k_s, v_s = _odd_proj(ys, norm_mix[layer], *rest, tabs_s, srows)

            mp = _mla_prompt(q_p, k_p, v_p, MLA_QBLOCK)
            gp = _gmlp_prompt(up, vp, w_spatial[o], b_spatial[o], PROMPT_ROWS)
            gs = _gmlp_sample(us[0], vs[0], w_spatial[o], b_spatial[o], nb, nt)[None]

            half = ROPE_C // 2
            inv_freq = jnp.power(jnp.float32(ROPE_THETA), -jnp.arange(half, dtype=F32) * (2.0 / ROPE_C))
            ang = jnp.arange(past).astype(F32)[:, None] * inv_freq[None, :]
            ctab = jnp.tile(jnp.cos(ang), (1, 2)).T
            stab = jnp.tile(jnp.sin(ang), (1, 2)).T
            ms8 = _mla_sample(page_table, pad_rows(to_seq(q_s), SUBLANES), pad_rows(to_seq(k_s), PAGE_SIZE),
                              pad_rows(to_seq(v_s), PAGE_SIZE), wukp, wuk, wuv, kn, ctab, stab,
                              cache_mla_ckv, jnp.transpose(cache_mla_kpe, (0, 1, 3, 2)), o, nt)
            ms = to_rows(ms8[:, :nt])

            wa = w_out_o[o][:HC * VD_C].astype(BF)
            wb = w_out_o[o][HC * VD_C:].astype(BF)
            mix_p, mix_s = (mp, gp), (ms, gs)

            pckv.append(ckvp)
            pkpe.append(kpep)
            sckv.append(to_seq(ckvs))
            skpe.append(to_seq(kpes))
            sgv.append(to_seq(vs))

        keep = FFN_CONV_WIDTH - 1
        ffn = (norm_ffn[layer], w_up[layer].astype(BF), ffn_conv_w[layer], ffn_conv_b[layer],
               w_down[layer].astype(BF))
        yp, tail_p = _ffn(yp, *mix_p, wa, wb, *ffn, jnp.zeros((bsz, SUBLANES, 2 * D_FF), F32), PROMPT_ROWS, 1)
        ys, tail_s = _ffn(ys, *mix_s, wa, wb, *ffn, to_rows(state_ffn_conv[layer]), srows, nb)
        pff.append(tail_p[:, SUBLANES - keep:, :])
        sff.append(to_seq(tail_s))

    return (yp, to_seq(ys),
            jnp.stack(pk), jnp.stack(pv), jnp.stack(pki), jnp.stack(pca),
            jnp.stack(pckv), jnp.stack(pkpe), jnp.stack(pff),
            jnp.stack(sk), jnp.stack(sv), jnp.stack(ski), jnp.stack(sca),
            jnp.stack(sckv), jnp.stack(skpe), jnp.stack(sgv), jnp.stack(sff))
```
